```python
import functools
import jax
import jax.numpy as jnp
from jax import lax
import numpy as np

D_MODEL = 4096
BATCH = 4
SEQ = 2048
DEPTH = 1
DEC_BATCH = 128
DEC_SEQ = 1
PAST_LEN = 16384
PAGE_SIZE = 128

N_META = 16
N_HEADS = 32
Q_LORA = 1024
KV_LORA = 512
QK_NOPE = 128
QK_ROPE = 64
V_HEAD = 128
ROPE_BASE = 10000.0
ATTN_SCALE = (QK_NOPE + QK_ROPE) ** -0.5
Q_BLOCK = 128
POOL_WIDTH = D_MODEL // 2
POOL_WINDOWS = (2, 4, 8, 16)
POOL_GROUP = POOL_WIDTH // len(POOL_WINDOWS)
POOL_STATE = max(POOL_WINDOWS) - 1
D_FF = 256 * ((8 * D_MODEL // 3 + 255) // 256)
CONV_W = 3
EPS = 1e-6
OFF_Q = POOL_WIDTH
OFF_KV = OFF_Q + Q_LORA
OFF_KR = OFF_KV + KV_LORA
OFF_GP = OFF_KR + QK_ROPE
OFF_GM = OFF_GP + D_MODEL
IN_COLS = OFF_GM + D_MODEL

kernel_name = 'hybrid_pool_mla_convffn_step'


def _rmsnorm(x, g):
    xf = x.astype(jnp.float32)
    y = xf * lax.rsqrt(jnp.mean(xf * xf, axis=-1, keepdims=True) + EPS)
    return (y * g.astype(jnp.float32)).astype(x.dtype)


def _rope_tables(pos):
    inv = ROPE_BASE ** (-jnp.arange(0, QK_ROPE, 2, dtype=jnp.float32) / QK_ROPE)
    ang = pos.astype(jnp.float32)[:, None] * inv[None, :]
    return jnp.cos(ang), jnp.sin(ang)


def _rope(x, cos, sin):
    shape = (1, cos.shape[0]) + (1,) * (x.ndim - 3) + (cos.shape[1],)
    c = cos.reshape(shape).astype(x.dtype)
    s = sin.reshape(shape).astype(x.dtype)
    x1, x2 = jnp.split(x, 2, axis=-1)
    return jnp.concatenate([x1 * c - x2 * s, x1 * s + x2 * c], axis=-1)


def _latent_attention(q_lat, q_pe, c_kv, k_pe, mask):
    s = (jnp.einsum('...qhc,...kc->...hqk', q_lat, c_kv)
         + jnp.einsum('...qhr,...kr->...hqk', q_pe, k_pe)).astype(jnp.float32) * ATTN_SCALE
    s = jnp.where(mask, s, -jnp.inf)
    p = jax.nn.softmax(s, axis=-1).astype(c_kv.dtype)
    return jnp.einsum('...hqk,...kc->...qhc', p, c_kv)


def _prompt_attention(q_nope, q_pe, c_kv, k_pe, w_uk, w_uv):
    B, L = c_kv.shape[:2]
    n_blk = -(-L // Q_BLOCK)
    Lp = n_blk * Q_BLOCK
    pad = Lp - L
    padt = lambda a: jnp.pad(a, [(0, 0), (0, pad)] + [(0, 0)] * (a.ndim - 2))
    c_p, kpe_p = padt(c_kv), padt(k_pe)
    qn = padt(q_nope).reshape(B, n_blk, Q_BLOCK, N_HEADS, QK_NOPE).swapaxes(0, 1)
    qp = padt(q_pe).reshape(B, n_blk, Q_BLOCK, N_HEADS, QK_ROPE).swapaxes(0, 1)
    k_pos = jnp.arange(Lp)

    def block(args):
        i, qn_b, qp_b = args
        q_lat = jnp.einsum('bqhd,chd->bqhc', qn_b, w_uk)
        q_pos = i * Q_BLOCK + jnp.arange(Q_BLOCK)
        mask = k_pos[None, :] <= q_pos[:, None]
        o_lat = _latent_attention(q_lat, qp_b, c_p, kpe_p, mask)
        o = jnp.einsum('bqhc,chv->bqhv', o_lat, w_uv)
        return o.reshape(B, Q_BLOCK, N_HEADS * V_HEAD)

    o = lax.map(block, (jnp.arange(n_blk), qn, qp))
    return o.swapaxes(0, 1).reshape(B, Lp, N_HEADS * V_HEAD)[:, :L]


def _sample_attention(q_nope, q_pe, c_new, kpe_new, cache_ckv, cache_kpe, page_table, w_uk, w_uv):
    B, T = c_new.shape[:2]
    past = page_table.shape[1] * cache_ckv.shape[1]
    q_lat = jnp.einsum('bqhd,chd->bqhc', q_nope, w_uk)
    k_idx = jnp.arange(past + T)
    q_idx = past + jnp.arange(T)
    mask = k_idx[None, :] <= q_idx[:, None]

    def one(args):
        pt, ql, qp, cn, kn = args
        c = jnp.concatenate([cache_ckv[pt].reshape(past, KV_LORA), cn.astype(cache_ckv.dtype)], axis=0)
        k = jnp.concatenate([cache_kpe[pt].reshape(past, QK_ROPE), kn.astype(cache_kpe.dtype)], axis=0)
        o_lat = _latent_attention(ql, qp, c.astype(ql.dtype), k.astype(qp.dtype), mask)
        return jnp.einsum('qhc,chv->qhv', o_lat, w_uv).reshape(T, N_HEADS * V_HEAD)

    return lax.map(one, (page_table, q_lat, q_pe, c_new, kpe_new))


def _pool_mix(u_ext, n_prev, w_grp, scale):
    B, N, W = u_ext.shape
    uf = u_ext.astype(jnp.float32)
    cs = jnp.concatenate([jnp.zeros((B, 1, W), jnp.float32), jnp.cumsum(uf, axis=1)], axis=1)
    rows = jnp.arange(n_prev, N)
    outs = []
    for g, win in enumerate(POOL_WINDOWS):
        lo_c, hi_c = g * POOL_GROUP, (g + 1) * POOL_GROUP
        hi = cs[:, rows + 1, lo_c:hi_c]
        lo = cs[:, jnp.maximum(rows + 1 - win, 0), lo_c:hi_c]
        cnt = jnp.minimum(rows + 1, win).astype(jnp.float32)[None, :, None]
        mix = (hi - lo) / cnt - uf[:, n_prev:, lo_c:hi_c]
        outs.append(jnp.einsum('btc,cd->btd', mix.astype(u_ext.dtype), w_grp[g]))
    return jnp.concatenate(outs, axis=-1) * scale


def _conv_ffn(h, prev, w_up, conv_w, conv_b, w_down):
    up = h @ w_up
    T = up.shape[1]
    ext = jnp.concatenate([prev.astype(up.dtype), up], axis=1)
    conv = conv_b + sum(ext[:, k:k + T] * conv_w[k] for k in range(CONV_W))
    gate, val = jnp.split(conv, 2, axis=-1)
    return (jax.nn.silu(gate) * val) @ w_down, ext[:, -(CONV_W - 1):]


def _layer(x, cos, sin, pool_prev, conv_prev, attend,
           norm_mix, w_in, g_qnorm, w_uq, g_kvnorm, w_attn_branch,
           w_pool_grp, pool_scale, w_pool_branch, w_out,
           norm_ffn, w_up, conv_w, conv_b, w_down):
    B, T, _ = x.shape
    h = _rmsnorm(x, norm_mix)
    u = h @ w_in
    u_pool = u[..., :OFF_Q]
    u_q = u[..., OFF_Q:OFF_KV]
    u_kv = u[..., OFF_KV:OFF_KR]
    u_kr = u[..., OFF_KR:OFF_GP]
    u_gp = u[..., OFF_GP:OFF_GM]
    u_gm = u[..., OFF_GM:]
    pool_ext = jnp.concatenate([pool_prev.astype(u.dtype), u_pool], axis=1)
    y_pool = _pool_mix(pool_ext, pool_prev.shape[1], w_pool_grp, pool_scale) @ w_pool_branch
    c_q = _rmsnorm(u_q, g_qnorm)
    q = (c_q @ w_uq).reshape(B, T, N_HEADS, QK_NOPE + QK_ROPE)
    q_nope = q[..., :QK_NOPE]
    q_pe = _rope(q[..., QK_NOPE:], cos, sin)
    c_kv = _rmsnorm(u_kv, g_kvnorm)
    k_pe = _rope(u_kr, cos, sin)
    y_attn = attend(q_nope, q_pe, c_kv, k_pe) @ w_attn_branch
    mixed = jax.nn.sigmoid(u_gp) * y_pool + jax.nn.sigmoid(u_gm) * y_attn
    x = x + mixed @ w_out
    ffn, conv_state = _conv_ffn(_rmsnorm(x, norm_ffn), conv_prev, w_up, conv_w, conv_b, w_down)
    x = x + ffn
    return x, c_kv, k_pe, pool_ext[:, -POOL_STATE:], conv_state


def setup_inputs(seed: int = 0) -> dict:
    key = jax.random.key(seed)
    ks = jax.random.split(key, 32)
    f32 = jnp.float32
    n_pages = PAST_LEN // PAGE_SIZE
    n_used = DEC_BATCH * n_pages
    n_pool = n_used + n_used // 4

    def w(k, shape, fan_in):
        return jax.random.normal(k, shape, f32) * fan_in ** -0.5

    def gain(k, shape):
        return 1.0 + 0.05 * jax.random.normal(k, shape, f32)

    page_table = jax.random.permutation(ks[6], n_pool)[:n_used].reshape(DEC_BATCH, n_pages).astype(jnp.int32)
    return {
        'x_prompt': jax.random.normal(ks[0], (BATCH, SEQ, D_MODEL), f32),
        'x_sample': jax.random.normal(ks[1], (DEC_BATCH, DEC_SEQ, D_MODEL), f32),
        'cache_ckv': jax.random.normal(ks[2], (DEPTH, n_pool, PAGE_SIZE, KV_LORA), f32),
        'cache_kpe': jax.random.normal(ks[3], (DEPTH, n_pool, PAGE_SIZE, QK_ROPE), f32),
        'state_pool': jax.random.normal(ks[4], (DEPTH, DEC_BATCH, POOL_STATE, POOL_WIDTH), f32),
        'state_conv': jax.random.normal(ks[5], (DEPTH, DEC_BATCH, CONV_W - 1, 2 * D_FF), f32),
        'page_table': page_table,
        'meta_tokens': jax.random.normal(ks[7], (N_META, D_MODEL), f32),
        'norm_mix': gain(ks[8], (DEPTH, D_MODEL)),
        'w_in': w(ks[9], (DEPTH, D_MODEL, IN_COLS), D_MODEL),
        'g_qnorm': gain(ks[10], (DEPTH, Q_LORA)),
        'w_uq': w(ks[11], (DEPTH, Q_LORA, N_HEADS * (QK_NOPE + QK_ROPE)), Q_LORA),
        'g_kvnorm': gain(ks[12], (DEPTH, KV_LORA)),
        'w_uk': w(ks[13], (DEPTH, KV_LORA, N_HEADS, QK_NOPE), KV_LORA),
        'w_uv': w(ks[14], (DEPTH, KV_LORA, N_HEADS, V_HEAD), KV_LORA),
        'w_attn_branch': w(ks[15], (DEPTH, N_HEADS * V_HEAD, D_MODEL), N_HEADS * V_HEAD),
        'w_pool_grp': w(ks[16], (DEPTH, len(POOL_WINDOWS), POOL_GROUP, POOL_GROUP), POOL_GROUP),
        'pool_scale': 1.0 + 0.1 * jax.random.normal(ks[17], (DEPTH, POOL_WIDTH), f32),
        'w_pool_branch': w(ks[18], (DEPTH, POOL_WIDTH, D_MODEL), POOL_WIDTH),
        'w_out': w(ks[19], (DEPTH, D_MODEL, D_MODEL), D_MODEL),
        'norm_ffn': gain(ks[20], (DEPTH, D_MODEL)),
        'w_up': w(ks[21], (DEPTH, D_MODEL, 2 * D_FF), D_MODEL),
        'conv_w': w(ks[22], (DEPTH, CONV_W, 2 * D_FF), CONV_W),
        'conv_b': 0.01 * jax.random.normal(ks[23], (DEPTH, 2 * D_FF), f32),
        'w_down': w(ks[24], (DEPTH, D_FF, D_MODEL), D_FF),
        'norm_final': gain(ks[25], (D_MODEL,)),
    }


def reference(x_prompt, x_sample, cache_ckv, cache_kpe, state_pool, state_conv, page_table,
              meta_tokens, norm_mix, w_in, g_qnorm, w_uq, g_kvnorm, w_uk, w_uv, w_attn_branch,
              w_pool_grp, pool_scale, w_pool_branch, w_out, norm_ffn, w_up, conv_w, conv_b,
              w_down, norm_final):
    Bp = x_prompt.shape[0]
    xp = jnp.concatenate([jnp.broadcast_to(meta_tokens[None].astype(x_prompt.dtype), (Bp, N_META, D_MODEL)), x_prompt], axis=1)
    L = xp.shape[1]
    xs = x_sample
    T = xs.shape[1]
    past = page_table.shape[1] * cache_ckv.shape[2]
    cos_p, sin_p = _rope_tables(jnp.arange(L))
    cos_s, sin_s = _rope_tables(past + jnp.arange(T))
    ckv_p, kpe_p, pool_p, conv_p = [], [], [], []
    ckv_s, kpe_s, pool_s, conv_s = [], [], [], []
    for l in range(DEPTH):
        shared = (norm_mix[l], w_in[l], g_qnorm[l], w_uq[l], g_kvnorm[l], w_attn_branch[l],
                  w_pool_grp[l], pool_scale[l], w_pool_branch[l], w_out[l],
                  norm_ffn[l], w_up[l], conv_w[l], conv_b[l], w_down[l])
        attend_p = functools.partial(_prompt_attention, w_uk=w_uk[l], w_uv=w_uv[l])
        attend_s = functools.partial(_sample_attention, cache_ckv=cache_ckv[l], cache_kpe=cache_kpe[l],
                                     page_table=page_table, w_uk=w_uk[l], w_uv=w_uv[l])
        xp, c, k, ps, cs = _layer(xp, cos_p, sin_p, jnp.zeros((Bp, 0, POOL_WIDTH), xp.dtype),
                                  jnp.zeros((Bp, CONV_W - 1, 2 * D_FF), xp.dtype), attend_p, *shared)
        ckv_p.append(c); kpe_p.append(k); pool_p.append(ps); conv_p.append(cs)
        xs, c, k, ps, cs = _layer(xs, cos_s, sin_s, state_pool[l], state_conv[l], attend_s, *shared)
        ckv_s.append(c); kpe_s.append(k); pool_s.append(ps); conv_s.append(cs)
    y_prompt = _rmsnorm(xp, norm_final)[:, N_META:]
    y_sample = _rmsnorm(xs, norm_final)
    return (y_prompt, y_sample,
            jnp.stack(ckv_p), jnp.stack(kpe_p), jnp.stack(pool_p), jnp.stack(conv_p),
            jnp.stack(ckv_s), jnp.stack(kpe_s), jnp.stack(pool_s), jnp.stack(conv_s))
```

```python
import functools

import jax
import jax.numpy as jnp
from jax import lax
from jax.experimental import pallas as pl
from jax.experimental.pallas import tpu as pltpu

EPS = 1e-6
ROPE_BASE = 10000.0
POOL_WINDOWS = (2, 4, 8, 16)
LANES = 128
META_PAD = 128
POOL_ZERO_ROWS = 16
MASK_VALUE = -1e30
VMEM_LIMIT_BYTES = 60 * 1024 * 1024

F32 = jnp.float32
BF16 = jnp.bfloat16


def _params(*sem):
    return pltpu.CompilerParams(dimension_semantics=sem, vmem_limit_bytes=VMEM_LIMIT_BYTES)


def _pick_block(n, cap, mult):
    best = None
    for d in range(mult, min(n, cap) + 1, mult):
        if n % d == 0:
            best = d
    assert best is not None, (n, cap, mult)
    return best


def _rms(x, g):
    return x * lax.rsqrt(jnp.mean(x * x, axis=-1, keepdims=True) + EPS) * g


def _norm_in_kernel(xt_ref, xe_ref, g_ref, x_ref, h_ref, *, n_tok_blocks):
    i = pl.program_id(0)

    def emit(src):
        x = src[...]
        x_ref[...] = x
        h_ref[...] = _rms(x, g_ref[...]).astype(h_ref.dtype)

    @pl.when(i < n_tok_blocks)
    def _():
        emit(xt_ref)

    @pl.when(i >= n_tok_blocks)
    def _():
        emit(xe_ref)


def _norm_in(x_tok, x_extra, g, rb):
    rt, d = x_tok.shape
    e = x_extra.shape[0]
    m = rt + e
    ntb, neb = rt // rb, e // rb
    return pl.pallas_call(
        functools.partial(_norm_in_kernel, n_tok_blocks=ntb),
        grid=(ntb + neb,),
        in_specs=[
            pl.BlockSpec((rb, d), lambda i: (jnp.minimum(i, ntb - 1), 0)),
            pl.BlockSpec((rb, d), lambda i: (jnp.maximum(i - ntb, 0), 0)),
            pl.BlockSpec((1, d), lambda i: (0, 0)),
        ],
        out_specs=[pl.BlockSpec((rb, d), lambda i: (i, 0)),
                   pl.BlockSpec((rb, d), lambda i: (i, 0))],
        out_shape=[jax.ShapeDtypeStruct((m, d), F32), jax.ShapeDtypeStruct((m, d), BF16)],
        compiler_params=_params("arbitrary"),
        name="norm_in",
    )(x_tok, x_extra, g.reshape(1, d))


def _norm_rows_kernel(x_ref, g_ref, o_ref):
    o_ref[...] = _rms(x_ref[...], g_ref[...]).astype(o_ref.dtype)


def _norm_rows(x, g, *, row0, nrows, rb, out_dtype, name):
    d = x.shape[1]
    assert row0 % rb == 0 and nrows % rb == 0
    b0 = row0 // rb
    return pl.pallas_call(
        _norm_rows_kernel,
        grid=(nrows // rb,),
        in_specs=[pl.BlockSpec((rb, d), lambda i: (b0 + i, 0)),
                  pl.BlockSpec((1, d), lambda i: (0, 0))],
        out_specs=pl.BlockSpec((rb, d), lambda i: (i, 0)),
        out_shape=jax.ShapeDtypeStruct((nrows, d), out_dtype),
        compiler_params=_params("arbitrary"),
        name=name,
    )(x, g.reshape(1, d))


def _mm_kernel(*refs, n_pro, n_extra, prologue, epilogue):
    x_ref, w_ref = refs[0], refs[1]
    pro_refs = refs[2:2 + n_pro]
    extra_refs = refs[2 + n_pro:2 + n_pro + n_extra]
    o_ref = refs[2 + n_pro + n_extra]
    if prologue is not None:
        xs_ref = refs[3 + n_pro + n_extra]

        @pl.when(pl.program_id(1) == 0)
        def _():
            xs_ref[...] = prologue(x_ref, *pro_refs).astype(xs_ref.dtype)

        xv = xs_ref[...]
    else:
        xv = x_ref[...]
    acc = jnp.dot(xv, w_ref[...], preferred_element_type=F32)
    epilogue(acc, o_ref, *extra_refs)


def _store_epilogue(acc, o_ref):
    o_ref[...] = acc.astype(o_ref.dtype)


def _matmul(x, w, *, m, k, n, bm, bn, out_cols, out_bn, out_dtype, name,
            x_col_block=0, pro=(), pro_specs=(), prologue=None,
            extra=(), extra_specs=(), epilogue=_store_epilogue):
    assert m % bm == 0 and n % bn == 0
    scratch = [pltpu.VMEM((bm, k), BF16)] if prologue is not None else []
    kern = functools.partial(_mm_kernel, n_pro=len(pro), n_extra=len(extra),
                             prologue=prologue, epilogue=epilogue)
    return pl.pallas_call(
        kern,
        grid=(m // bm, n // bn),
        in_specs=[pl.BlockSpec((bm, k), lambda i, j: (i, x_col_block)),
                  pl.BlockSpec((k, bn), lambda i, j: (0, j)),
                  *pro_specs, *extra_specs],
        out_specs=pl.BlockSpec((bm, out_bn), lambda i, j: (i, j)),
        out_shape=jax.ShapeDtypeStruct((m, out_cols), out_dtype),
        scratch_shapes=scratch,
        compiler_params=_params("parallel", "arbitrary"),
        name=name,
    )(x, w, *pro, *extra)


def _rope_tile(v, ct, s1, s2, half):
    return v * ct + pltpu.roll(v, LANES - half, 1) * s1 + pltpu.roll(v, half, 1) * s2


def _kv_prep_kernel(ukv_ref, ukr_ref, g_ref, ct_ref, s1_ref, s2_ref, c32_ref, c16_ref, kpe_ref, *, half):
    c = _rms(ukv_ref[...], g_ref[...])
    c32_ref[...] = c
    c16_ref[...] = c.astype(BF16)
    kpe_ref[...] = _rope_tile(ukr_ref[...], ct_ref[...], s1_ref[...], s2_ref[...], half)


def _softmax_step(q, k, v, carry, mask=None):
    m_i, l_i, acc = carry
    s = lax.dot_general(q, k, (((1,), (1,)), ((), ())), preferred_element_type=F32)
    if mask is not None:
        s = jnp.where(mask, s, MASK_VALUE)
    m_new = jnp.maximum(m_i, jnp.max(s, axis=1, keepdims=True))
    alpha = jnp.exp(m_i - m_new)
    p = jnp.exp(s - m_new)
    l_new = alpha * l_i + jnp.sum(p, axis=1, keepdims=True)
    acc_new = alpha * acc + jnp.dot(p.astype(v.dtype), v, preferred_element_type=F32)
    return m_new, l_new, acc_new


def _attn_tok_kernel(q_ref, k_ref, v_ref, km_ref, vm_ref, o_ref, *, bq, bk, n_meta):
    qi = pl.program_id(2)
    q = q_ref[...]
    vdim = v_ref.shape[1]
    carry = (jnp.full((bq, 1), MASK_VALUE, F32), jnp.zeros((bq, 1), F32), jnp.zeros((bq, vdim), F32))
    meta_mask = lax.broadcasted_iota(jnp.int32, (bq, km_ref.shape[0]), 1) < n_meta
    carry = _softmax_step(q, km_ref[...], vm_ref[...], carry, meta_mask)
    per = bq // bk

    def body(j, c):
        start = pl.multiple_of(j * bk, bk)
        return _softmax_step(q, k_ref[pl.ds(start, bk), :], v_ref[pl.ds(start, bk), :], c)

    carry = lax.fori_loop(0, qi * per, body, carry)
    row = lax.broadcasted_iota(jnp.int32, (bq, bk), 0)
    col = lax.broadcasted_iota(jnp.int32, (bq, bk), 1)
    for d in range(per):
        start = pl.multiple_of((qi * per + d) * bk, bk)
        carry = _softmax_step(q, k_ref[pl.ds(start, bk), :], v_ref[pl.ds(start, bk), :], carry,
                              col + d * bk <= row)
    _, l_i, acc = carry
    o_ref[...] = (acc / l_i).astype(o_ref.dtype)


def _attn_meta_kernel(q_ref, km_ref, vm_ref, o_any, o_ref, *, n_meta):
    del o_any
    n = q_ref.shape[0]
    row = lax.broadcasted_iota(jnp.int32, (n, n), 0)
    col = lax.broadcasted_iota(jnp.int32, (n, n), 1)
    mask = (col <= row) & (col < n_meta)
    carry = (jnp.full((n, 1), MASK_VALUE, F32), jnp.zeros((n, 1), F32),
             jnp.zeros((n, vm_ref.shape[1]), F32))
    _, l_i, acc = _softmax_step(q_ref[...], km_ref[...], vm_ref[...], carry, mask)
    o_ref[...] = (acc / l_i).astype(o_ref.dtype)


def _decode_kernel(pt_ref, *refs, pages, page_size, c_dim, r_dim):
    del pt_ref
    ckv_refs = refs[:pages]
    kpe_refs = refs[pages:2 * pages]
    q_ref, new_ref, o_ref, kc_ref, m_ref, l_ref, acc_ref = refs[2 * pages:]
    b, j = pl.program_id(0), pl.program_id(1)
    qk_dim = kc_ref.shape[1]

    @pl.when((b == 0) & (j == 0))
    def _():
        kc_ref[:, c_dim + r_dim:] = jnp.zeros((kc_ref.shape[0], qk_dim - c_dim - r_dim), kc_ref.dtype)

    q = q_ref[0]

    @pl.when(j == 0)
    def _():
        new = new_ref[0]
        s_self = jnp.sum(q.astype(F32) * new, axis=1, keepdims=True)
        m_ref[...] = s_self
        l_ref[...] = jnp.ones_like(s_self)
        acc_ref[...] = jnp.broadcast_to(new[:, :c_dim], acc_ref.shape)

    for p in range(pages):
        rows = pl.ds(p * page_size, page_size)
        kc_ref[rows, :c_dim] = ckv_refs[p][0].astype(kc_ref.dtype)
        kc_ref[rows, c_dim:c_dim + r_dim] = kpe_refs[p][0].astype(kc_ref.dtype)

    kc = kc_ref[...]
    s = lax.dot_general(q, kc, (((1,), (1,)), ((), ())), preferred_element_type=F32)
    m_old = m_ref[...]
    m_new = jnp.maximum(m_old, jnp.max(s, axis=1, keepdims=True))
    alpha = jnp.exp(m_old - m_new)
    p_att = jnp.exp(s - m_new)
    l_ref[...] = alpha * l_ref[...] + jnp.sum(p_att, axis=1, keepdims=True)
    acc_ref[...] = alpha * acc_ref[...] + jnp.dot(p_att.astype(kc.dtype), kc_ref[:, :c_dim],
                                                   preferred_element_type=F32)
    m_ref[...] = m_new

    @pl.when(j == pl.num_programs(1) - 1)
    def _():
        o_ref[0] = (acc_ref[...] / l_ref[...]).astype(o_ref.dtype)


def _decode_attention(page_table, cache_ckv, cache_kpe, q_cat, new_cat, *, pages):
    db, n_pages = page_table.shape
    _, page_size, c_dim = cache_ckv.shape
    r_dim = cache_kpe.shape[2]
    n_heads, qk_dim = q_cat.shape[1], q_cat.shape[2]
    assert n_pages % pages == 0

    def page_spec(width, p):
        return pl.BlockSpec((1, page_size, width), lambda b, j, pt: (pt[b, j * pages + p], 0, 0))

    grid_spec = pltpu.PrefetchScalarGridSpec(
        num_scalar_prefetch=1,
        grid=(db, n_pages // pages),
        in_specs=[*[page_spec(c_dim, p) for p in range(pages)],
                  *[page_spec(r_dim, p) for p in range(pages)],
                  pl.BlockSpec((1, n_heads, qk_dim), lambda b, j, pt: (b, 0, 0)),
                  pl.BlockSpec((1, 1, qk_dim), lambda b, j, pt: (b, 0, 0))],
        out_specs=pl.BlockSpec((1, n_heads, c_dim), lambda b, j, pt: (b, 0, 0)),
        scratch_shapes=[pltpu.VMEM((pages * page_size, qk_dim), BF16),
                        pltpu.VMEM((n_heads, 1), F32),
                        pltpu.VMEM((n_heads, 1), F32),
                        pltpu.VMEM((n_heads, c_dim), F32)],
    )
    kern = functools.partial(_decode_kernel, pages=pages, page_size=page_size, c_dim=c_dim, r_dim=r_dim)
    return pl.pallas_call(
        kern,
        grid_spec=grid_spec,
        out_shape=jax.ShapeDtypeStruct((db, n_heads, c_dim), F32),
        compiler_params=_params("arbitrary", "arbitrary"),
        name="decode_attention",
    )(page_table, *([cache_ckv] * pages), *([cache_kpe] * pages), q_cat, new_cat)


def _qlat_kernel(q_ref, w_ref, o_ref):
    o_ref[0] = jnp.dot(q_ref[...], w_ref[0], preferred_element_type=F32)


def _ouv_kernel(x_ref, w_ref, o_any, o_ref):
    del o_any
    o_ref[...] = jnp.dot(x_ref[0], w_ref[...], preferred_element_type=F32).astype(o_ref.dtype)


def _pool_tok_kernel(u_ref, um_ref, w_ref, sc_ref, o_ref, z_ref, t_ref, *, n_meta):
    g = pl.program_id(1)
    s_len = u_ref.shape[0]
    zp = POOL_ZERO_ROWS
    end = zp + n_meta + s_len
    z_ref[:zp, :] = jnp.zeros((zp, z_ref.shape[1]), F32)
    t_ref[:zp, :] = jnp.zeros((zp, t_ref.shape[1]), F32)
    z_ref[zp:zp + n_meta, :] = um_ref[...]
    z_ref[zp + n_meta:, :] = u_ref[...]

    def window_sum(win):
        src, dst = z_ref, t_ref
        d = 1
        while d < win:
            dst[zp:end, :] = src[zp:end, :] + src[zp - d:end - d, :]
            src, dst = dst, src
            d *= 2
        return src[zp + n_meta:end, :]

    for gi, win in enumerate(POOL_WINDOWS):
        @pl.when(g == gi)
        def _(win=win):
            mix = window_sum(win) / float(win) - u_ref[...]
            y = jnp.dot(mix.astype(BF16), w_ref[0], preferred_element_type=F32)
            o_ref[...] = (y * sc_ref[...]).astype(o_ref.dtype)


def _pool_extra_kernel(st_ref, us_ref, um_ref, w_ref, sc_ref, o_any, o_ref, acc_ref, zz_ref,
                       *, n_state, pg, n_meta_pad):
    del o_any
    j = pl.program_id(0)

    @pl.when(j == 0)
    def _():
        acc_ref[...] = jnp.zeros_like(acc_ref)

    for gi, win in enumerate(POOL_WINDOWS):
        cols = slice(gi * pg, (gi + 1) * pg)

        @pl.when(n_state - j <= win - 1)
        def _(cols=cols):
            acc_ref[:, cols] += st_ref[:, cols]

    @pl.when(j == n_state - 1)
    def _():
        zpad = zz_ref.shape[0] - n_meta_pad
        zz_ref[:zpad, :] = jnp.zeros((zpad, zz_ref.shape[1]), F32)
        zz_ref[zpad:, :] = um_ref[...]
        t = lax.broadcasted_iota(jnp.int32, (n_meta_pad, 1), 0)
        for gi, win in enumerate(POOL_WINDOWS):
            cols = slice(gi * pg, (gi + 1) * pg)
            us = us_ref[:, cols]
            mix_s = (acc_ref[:, cols] + us) / float(win) - us
            wsum = zz_ref[zpad:, cols]
            for kk in range(1, win):
                wsum = wsum + zz_ref[zpad - kk:zpad - kk + n_meta_pad, cols]
            cnt = jnp.minimum(t + 1, win).astype(F32)
            mix_m = wsum / cnt - um_ref[:, cols]
            w = w_ref[gi]
            sc = sc_ref[:, cols]
            o_ref[:n_meta_pad, cols] = (jnp.dot(mix_m.astype(BF16), w, preferred_element_type=F32)
                                        * sc).astype(o_ref.dtype)
            o_ref[n_meta_pad:, cols] = (jnp.dot(mix_s.astype(BF16), w, preferred_element_type=F32)
                                        * sc).astype(o_ref.dtype)


def _conv3(x0, x1, x2, w_ref, b_ref):
    return b_ref[...] + x0 * w_ref[0:1, :] + x1 * w_ref[1:2, :] + x2 * w_ref[2:3, :]


def _conv_tok_kernel(ug_ref, uv_ref, mg_ref, mv_ref, wg_ref, wv_ref, bg_ref, bv_ref, o_ref,
                     zg_ref, zv_ref, *, n_meta):
    s_len = ug_ref.shape[0]

    def conv(u_ref, m_ref, z_ref, w_ref, b_ref):
        z_ref[:n_meta, :] = m_ref[...]
        z_ref[n_meta:, :] = u_ref[...]
        return _conv3(z_ref[n_meta - 2:n_meta - 2 + s_len, :], z_ref[n_meta - 1:n_meta - 1 + s_len, :],
                      u_ref[...], w_ref, b_ref)

    gate = conv(ug_ref, mg_ref, zg_ref, wg_ref, bg_ref)
    val = conv(uv_ref, mv_ref, zv_ref, wv_ref, bv_ref)
    o_ref[...] = (jax.nn.silu(gate) * val).astype(o_ref.dtype)


def _conv_extra_kernel(mg_ref, mv_ref, sg_ref, sv_ref, p0g_ref, p0v_ref, p1g_ref, p1v_ref,
                       wg_ref, wv_ref, bg_ref, bv_ref, o_any, o_ref, zg_ref, zv_ref, *, n_meta_pad):
    del o_any
    zpad = zg_ref.shape[0] - n_meta_pad

    def conv_meta(m_ref, z_ref, w_ref, b_ref):
        z_ref[:zpad, :] = jnp.zeros((zpad, z_ref.shape[1]), F32)
        z_ref[zpad:, :] = m_ref[...]
        return _conv3(z_ref[zpad - 2:zpad - 2 + n_meta_pad, :], z_ref[zpad - 1:zpad - 1 + n_meta_pad, :],
                      m_ref[...], w_ref, b_ref)

    gate_m = conv_meta(mg_ref, zg_ref, wg_ref, bg_ref)
    val_m = conv_meta(mv_ref, zv_ref, wv_ref, bv_ref)
    o_ref[:n_meta_pad, :] = (jax.nn.silu(gate_m) * val_m).astype(o_ref.dtype)
    gate_s = _conv3(p0g_ref[...], p1g_ref[...], sg_ref[...], wg_ref, bg_ref)
    val_s = _conv3(p0v_ref[...], p1v_ref[...], sv_ref[...], wv_ref, bv_ref)
    o_ref[n_meta_pad:, :] = (jax.nn.silu(gate_s) * val_s).astype(o_ref.dtype)


def kernel(x_prompt, x_sample, cache_ckv, cache_kpe, state_pool, state_conv, page_table, meta_tokens,
           norm_mix, w_in, g_qnorm, w_uq, g_kvnorm, w_uk, w_uv, w_attn_branch, w_pool_grp, pool_scale,
           w_pool_branch, w_out, norm_ffn, w_up, conv_w, conv_b, w_down, norm_final):
    nb, s_len, d = x_prompt.shape
    db, dec_t, _ = x_sample.shape
    depth = w_in.shape[0]
    assert depth == 1 and dec_t == 1
    nm = meta_tokens.shape[0]
    _, c_dim, n_heads, d_nope = w_uk.shape
    v_dim = w_uv.shape[3]
    r_dim = cache_kpe.shape[3]
    half = r_dim // 2
    q_lora = g_qnorm.shape[1]
    pw = pool_scale.shape[1]
    n_grp = len(POOL_WINDOWS)
    pg = pw // n_grp
    n_state = state_pool.shape[2]
    conv_taps = conv_w.shape[1]
    dff2 = w_up.shape[2]
    dff = dff2 // 2
    page_size = cache_ckv.shape[2]
    n_pages = page_table.shape[1]
    past = n_pages * page_size
    scale = float(d_nope + r_dim) ** -0.5
    hd = 2 * LANES
    assert d_nope == LANES and v_dim == LANES and r_dim <= LANES and conv_taps == 3
    assert nm >= max(POOL_WINDOWS) - 1 and nm % 16 == 0 and nm <= META_PAD
    assert n_state == max(POOL_WINDOWS) - 1

    rt = nb * s_len
    r_meta = rt
    r_samp = rt + META_PAD
    e_rows = META_PAD + db
    m_rows = rt + e_rows
    assert rt % META_PAD == 0 and r_samp % db == 0 and db % 16 == 0 and r_meta % nm == 0

    off_q, off_kv, off_kr, off_gp = pw, pw + q_lora, pw + q_lora + c_dim, pw + q_lora + c_dim + r_dim
    assert off_q % q_lora == 0 and off_kv % c_dim == 0 and off_kr % LANES == 0 and pw % pg == 0

    n_a = off_kr + LANES
    n_a_pad = -(-n_a // 256) * 256
    w_a = jnp.concatenate([w_in[0, :, :off_gp], jnp.zeros((d, n_a_pad - off_gp), F32)], axis=1).astype(BF16)
    w_g = w_in[0, :, off_gp:].astype(BF16)
    wq = w_uq[0].reshape(q_lora, n_heads, d_nope + r_dim)
    wq = jnp.concatenate([wq, jnp.zeros((q_lora, n_heads, hd - d_nope - r_dim), F32)], axis=2)
    wq = wq.reshape(q_lora, n_heads * hd).astype(BF16)
    w_uk2 = w_uk[0].reshape(c_dim, n_heads * d_nope).astype(BF16)
    w_ukt = jnp.transpose(w_uk[0], (1, 2, 0)).astype(BF16)
    w_uv2 = w_uv[0].reshape(c_dim, n_heads * v_dim).astype(BF16)
    w_ab = w_attn_branch[0].astype(BF16)
    w_pg = w_pool_grp[0].astype(BF16)
    w_pb = w_pool_branch[0].astype(BF16)
    w_o = w_out[0].astype(BF16)
    w_u = w_up[0].astype(BF16)
    w_d = w_down[0].astype(BF16)

    pos = jnp.concatenate([
        jnp.tile(nm + jnp.arange(s_len), nb),
        jnp.arange(nm), jnp.zeros((META_PAD - nm,), jnp.int32),
        jnp.full((db,), past, jnp.int32)])
    inv = ROPE_BASE ** (-jnp.arange(0, r_dim, 2, dtype=F32) / r_dim)
    ang = pos.astype(F32)[:, None] * inv[None, :]
    cos, sin = jnp.cos(ang), jnp.sin(ang)
    zl = jnp.zeros((m_rows, LANES - 2 * half), F32)
    zh = jnp.zeros((m_rows, half), F32)
    tab_c = jnp.concatenate([cos, cos, zl], axis=1)
    tab_s1 = jnp.concatenate([-sin, zh, zl], axis=1)
    tab_s2 = jnp.concatenate([zh, sin, zl], axis=1)

    bm = _pick_block(m_rows, 1100, 16)
    row_tab_spec = pl.BlockSpec((bm, LANES), lambda i, j: (i, 0))

    x_extra = jnp.concatenate([meta_tokens.astype(F32), jnp.zeros((META_PAD - nm, d), F32),
                               x_sample.reshape(db, d)], axis=0)
    rb = _pick_block(e_rows, 256, 16)
    assert rt % rb == 0
    x_all, h = _norm_in(x_prompt.reshape(rt, d), x_extra, norm_mix[0], rb)

    bn_a = _pick_block(n_a_pad, 768, LANES)
    u1 = _matmul(h, w_a, m=m_rows, k=d, n=n_a_pad, bm=bm, bn=bn_a, out_cols=n_a_pad, out_bn=bn_a,
                 out_dtype=F32, name="in_proj")

    def gate_epilogue(acc, o_ref):
        o_ref[...] = jax.nn.sigmoid(acc).astype(o_ref.dtype)

    bn_d = _pick_block(d, 512, LANES)
    gates = _matmul(h, w_g, m=m_rows, k=d, n=2 * d, bm=bm, bn=bn_d, out_cols=2 * d, out_bn=bn_d,
                    out_dtype=BF16, name="gate_proj", epilogue=gate_epilogue)

    def q_prologue(x_ref, g_ref):
        return _rms(x_ref[...], g_ref[...])

    bn_q = _pick_block(n_heads * hd, 512, hd)

    def q_epilogue(acc, o_ref, ct_ref, s1_ref, s2_ref):
        acc = acc * scale
        for cb in range(bn_q // LANES):
            cols = slice(cb * LANES, (cb + 1) * LANES)
            v = acc[:, cols]
            if cb % 2 == 1:
                v = _rope_tile(v, ct_ref[...], s1_ref[...], s2_ref[...], half)
            o_ref[:, cols] = v.astype(o_ref.dtype)

    q_all = _matmul(u1, wq, m=m_rows, k=q_lora, n=n_heads * hd, bm=bm, bn=bn_q,
                    out_cols=n_heads * hd, out_bn=bn_q, out_dtype=BF16, name="q_proj",
                    x_col_block=off_q // q_lora,
                    pro=(g_qnorm[0].reshape(1, q_lora),), pro_specs=(pl.BlockSpec((1, q_lora), lambda i, j: (0, 0)),),
                    prologue=q_prologue,
                    extra=(tab_c, tab_s1, tab_s2), extra_specs=(row_tab_spec,) * 3, epilogue=q_epilogue)

    tab1 = pl.BlockSpec((bm, LANES), lambda i: (i, 0))
    ckv32, ckv16, kpe32 = pl.pallas_call(
        functools.partial(_kv_prep_kernel, half=half),
        grid=(m_rows // bm,),
        in_specs=[pl.BlockSpec((bm, c_dim), lambda i: (i, off_kv // c_dim)),
                  pl.BlockSpec((bm, LANES), lambda i: (i, off_kr // LANES)),
                  pl.BlockSpec((1, c_dim), lambda i: (0, 0)),
                  tab1, tab1, tab1],
        out_specs=[pl.BlockSpec((bm, c_dim), lambda i: (i, 0)),
                   pl.BlockSpec((bm, c_dim), lambda i: (i, 0)),
                   pl.BlockSpec((bm, LANES), lambda i: (i, 0))],
        out_shape=[jax.ShapeDtypeStruct((m_rows, c_dim), F32),
                   jax.ShapeDtypeStruct((m_rows, c_dim), BF16),
                   jax.ShapeDtypeStruct((m_rows, LANES), F32)],
        compiler_params=_params("arbitrary"),
        name="kv_prep",
    )(u1, u1, g_kvnorm[0].reshape(1, c_dim), tab_c, tab_s1, tab_s2)

    hps = _pick_block(n_heads, 4, 1)

    def k_epilogue(acc, o_ref, kpe_ref):
        kr = kpe_ref[...].astype(o_ref.dtype)
        for hh in range(hps):
            o_ref[:, hh * hd:hh * hd + d_nope] = acc[:, hh * d_nope:(hh + 1) * d_nope].astype(o_ref.dtype)
            o_ref[:, hh * hd + d_nope:(hh + 1) * hd] = kr

    k_all = _matmul(ckv16, w_uk2, m=m_rows, k=c_dim, n=n_heads * d_nope, bm=bm, bn=hps * d_nope,
                    out_cols=n_heads * hd, out_bn=hps * hd, out_dtype=BF16, name="k_up",
                    extra=(kpe32,), extra_specs=(row_tab_spec,), epilogue=k_epilogue)
    bn_v = _pick_block(n_heads * v_dim, 512, LANES)
    v_all = _matmul(ckv16, w_uv2, m=m_rows, k=c_dim, n=n_heads * v_dim, bm=bm, bn=bn_v,
                    out_cols=n_heads * v_dim, out_bn=bn_v, out_dtype=BF16, name="v_up")

    bq = _pick_block(s_len, 512, 128)
    bk = bq
    mb = r_meta // META_PAD
    o_all = pl.pallas_call(
        functools.partial(_attn_tok_kernel, bq=bq, bk=bk, n_meta=nm),
        grid=(nb, n_heads, s_len // bq),
        in_specs=[pl.BlockSpec((bq, hd), lambda b, hh, qi: (b * (s_len // bq) + qi, hh)),
                  pl.BlockSpec((s_len, hd), lambda b, hh, qi: (b, hh)),
                  pl.BlockSpec((s_len, v_dim), lambda b, hh, qi: (b, hh)),
                  pl.BlockSpec((META_PAD, hd), lambda b, hh, qi: (mb, hh)),
                  pl.BlockSpec((META_PAD, v_dim), lambda b, hh, qi: (mb, hh))],
        out_specs=pl.BlockSpec((bq, v_dim), lambda b, hh, qi: (b * (s_len // bq) + qi, hh)),
        out_shape=jax.ShapeDtypeStruct((m_rows, n_heads * v_dim), BF16),
        compiler_params=_params("parallel", "parallel", "arbitrary"),
        name="attn_prompt",
    )(q_all, k_all, v_all, k_all, v_all)

    o_all = pl.pallas_call(
        functools.partial(_attn_meta_kernel, n_meta=nm),
        grid=(n_heads,),
        in_specs=[pl.BlockSpec((META_PAD, hd), lambda hh: (mb, hh)),
                  pl.BlockSpec((META_PAD, hd), lambda hh: (mb, hh)),
                  pl.BlockSpec((META_PAD, v_dim), lambda hh: (mb, hh)),
                  pl.BlockSpec(memory_space=pl.ANY)],
        out_specs=pl.BlockSpec((META_PAD, v_dim), lambda hh: (mb, hh)),
        out_shape=jax.ShapeDtypeStruct((m_rows, n_heads * v_dim), BF16),
        input_output_aliases={3: 0},
        compiler_params=_params("arbitrary"),
        name="attn_meta",
    )(q_all, k_all, v_all, o_all)

    sb = r_samp // db
    q_lat = pl.pallas_call(
        _qlat_kernel,
        grid=(n_heads,),
        in_specs=[pl.BlockSpec((db, d_nope), lambda hh: (sb, hh * (hd // d_nope))),
                  pl.BlockSpec((1, d_nope, c_dim), lambda hh: (hh, 0, 0))],
        out_specs=pl.BlockSpec((1, db, c_dim), lambda hh: (hh, 0, 0)),
        out_shape=jax.ShapeDtypeStruct((n_heads, db, c_dim), F32),
        compiler_params=_params("arbitrary"),
        name="q_absorb",
    )(q_all, w_ukt)
    qk_dim = -(-(c_dim + r_dim) // LANES) * LANES
    q_samp = q_all[r_samp:r_samp + db].reshape(db, n_heads, hd)
    q_cat = jnp.concatenate([jnp.transpose(q_lat, (1, 0, 2)).astype(BF16),
                             q_samp[:, :, d_nope:d_nope + r_dim],
                             jnp.zeros((db, n_heads, qk_dim - c_dim - r_dim), BF16)], axis=2)
    new_cat = jnp.concatenate([ckv32[r_samp:r_samp + db], kpe32[r_samp:r_samp + db, :r_dim],
                               jnp.zeros((db, qk_dim - c_dim - r_dim), F32)], axis=1).reshape(db, 1, qk_dim)
    pages = _pick_block(n_pages, 16, 1)
    o_lat = _decode_attention(page_table, cache_ckv[0], cache_kpe[0], q_cat, new_cat, pages=pages)
    o_lat_t = jnp.transpose(o_lat, (1, 0, 2)).astype(BF16)
    o_all = pl.pallas_call(
        _ouv_kernel,
        grid=(n_heads,),
        in_specs=[pl.BlockSpec((1, db, c_dim), lambda hh: (hh, 0, 0)),
                  pl.BlockSpec((c_dim, v_dim), lambda hh: (0, hh)),
                  pl.BlockSpec(memory_space=pl.ANY)],
        out_specs=pl.BlockSpec((db, v_dim), lambda hh: (sb, hh)),
        out_shape=jax.ShapeDtypeStruct((m_rows, n_heads * v_dim), BF16),
        input_output_aliases={2: 0},
        compiler_params=_params("arbitrary"),
        name="o_up_decode",
    )(o_lat_t, w_uv2, o_all)

    scale_row = pool_scale[0].reshape(1, pw)
    pm = pl.pallas_call(
        functools.partial(_pool_tok_kernel, n_meta=nm),
        grid=(nb, n_grp),
        in_specs=[pl.BlockSpec((s_len, pg), lambda b, g: (b, g)),
                  pl.BlockSpec((nm, pg), lambda b, g: (r_meta // nm, g)),
                  pl.BlockSpec((1, pg, pg), lambda b, g: (g, 0, 0)),
                  pl.BlockSpec((1, pg), lambda b, g: (0, g))],
        out_specs=pl.BlockSpec((s_len, pg), lambda b, g: (b, g)),
        out_shape=jax.ShapeDtypeStruct((m_rows, pw), BF16),
        scratch_shapes=[pltpu.VMEM((POOL_ZERO_ROWS + nm + s_len, pg), F32),
                        pltpu.VMEM((POOL_ZERO_ROWS + nm + s_len, pg), F32)],
        compiler_params=_params("parallel", "arbitrary"),
        name="pool_prompt",
    )(u1, u1, w_pg, scale_row)
    st2d = state_pool[0].reshape(db, n_state * pw)
    zrows = 16
    pm = pl.pallas_call(
        functools.partial(_pool_extra_kernel, n_state=n_state, pg=pg, n_meta_pad=META_PAD),
        grid=(n_state,),
        in_specs=[pl.BlockSpec((db, pw), lambda j: (0, j)),
                  pl.BlockSpec((db, pw), lambda j: (sb, 0)),
                  pl.BlockSpec((META_PAD, pw), lambda j: (mb, 0)),
                  pl.BlockSpec((n_grp, pg, pg), lambda j: (0, 0, 0)),
                  pl.BlockSpec((1, pw), lambda j: (0, 0)),
                  pl.BlockSpec(memory_space=pl.ANY)],
        out_specs=pl.BlockSpec((e_rows, pw), lambda j: (rt // e_rows, 0)),
        out_shape=jax.ShapeDtypeStruct((m_rows, pw), BF16),
        scratch_shapes=[pltpu.VMEM((db, pw), F32), pltpu.VMEM((zrows + META_PAD, pw), F32)],
        input_output_aliases={5: 0},
        compiler_params=_params("arbitrary"),
        name="pool_extra",
    )(st2d, u1, u1, w_pg, scale_row, pm)

    def pool_branch_epilogue(acc, o_ref, g_ref):
        o_ref[...] = (g_ref[...].astype(F32) * acc).astype(o_ref.dtype)

    tile_spec = pl.BlockSpec((bm, bn_d), lambda i, j: (i, j))
    t1 = _matmul(pm, w_pb, m=m_rows, k=pw, n=d, bm=bm, bn=bn_d, out_cols=d, out_bn=bn_d, out_dtype=F32,
                 name="pool_branch", extra=(gates,), extra_specs=(tile_spec,), epilogue=pool_branch_epilogue)

    def attn_branch_epilogue(acc, o_ref, g_ref, t_ref):
        o_ref[...] = (t_ref[...] + g_ref[...].astype(F32) * acc).astype(o_ref.dtype)

    mixed = _matmul(o_all, w_ab, m=m_rows, k=n_heads * v_dim, n=d, bm=bm, bn=bn_d, out_cols=d, out_bn=bn_d,
                    out_dtype=BF16, name="attn_branch", extra=(gates, t1),
                    extra_specs=(pl.BlockSpec((bm, bn_d), lambda i, j: (i, j + d // bn_d)), tile_spec),
                    epilogue=attn_branch_epilogue)

    def residual_epilogue(acc, o_ref, r_ref):
        o_ref[...] = r_ref[...] + acc

    x1 = _matmul(mixed, w_o, m=m_rows, k=d, n=d, bm=bm, bn=bn_d, out_cols=d, out_bn=bn_d, out_dtype=F32,
                 name="out_proj", extra=(x_all,), extra_specs=(tile_spec,), epilogue=residual_epilogue)

    rb2 = _pick_block(m_rows, 512, 16)
    h2 = _norm_rows(x1, norm_ffn[0], row0=0, nrows=m_rows, rb=rb2, out_dtype=BF16, name="norm_ffn")
    bn_u = _pick_block(dff2, 512, LANES)
    up = _matmul(h2, w_u, m=m_rows, k=d, n=dff2, bm=bm, bn=bn_u, out_cols=dff2, out_bn=bn_u, out_dtype=F32,
                 name="ffn_up")

    bc = _pick_block(dff, 256, LANES)
    ncb = dff // bc
    cw, cb = conv_w[0], conv_b[0].reshape(1, dff2)
    act = pl.pallas_call(
        functools.partial(_conv_tok_kernel, n_meta=nm),
        grid=(nb, ncb),
        in_specs=[pl.BlockSpec((s_len, bc), lambda b, j: (b, j)),
                  pl.BlockSpec((s_len, bc), lambda b, j: (b, j + ncb)),
                  pl.BlockSpec((nm, bc), lambda b, j: (r_meta // nm, j)),
                  pl.BlockSpec((nm, bc), lambda b, j: (r_meta // nm, j + ncb)),
                  pl.BlockSpec((conv_taps, bc), lambda b, j: (0, j)),
                  pl.BlockSpec((conv_taps, bc), lambda b, j: (0, j + ncb)),
                  pl.BlockSpec((1, bc), lambda b, j: (0, j)),
                  pl.BlockSpec((1, bc), lambda b, j: (0, j + ncb))],
        out_specs=pl.BlockSpec((s_len, bc), lambda b, j: (b, j)),
        out_shape=jax.ShapeDtypeStruct((m_rows, dff), BF16),
        scratch_shapes=[pltpu.VMEM((nm + s_len, bc), F32), pltpu.VMEM((nm + s_len, bc), F32)],
        compiler_params=_params("parallel", "arbitrary"),
        name="conv_prompt",
    )(up, up, up, up, cw, cw, cb, cb)
    sc2d = state_conv[0].reshape(db, (conv_taps - 1) * dff2)
    act = pl.pallas_call(
        functools.partial(_conv_extra_kernel, n_meta_pad=META_PAD),
        grid=(ncb,),
        in_specs=[pl.BlockSpec((META_PAD, bc), lambda j: (mb, j)),
                  pl.BlockSpec((META_PAD, bc), lambda j: (mb, j + ncb)),
                  pl.BlockSpec((db, bc), lambda j: (sb, j)),
                  pl.BlockSpec((db, bc), lambda j: (sb, j + ncb)),
                  pl.BlockSpec((db, bc), lambda j: (0, j)),
                  pl.BlockSpec((db, bc), lambda j: (0, j + ncb)),
                  pl.BlockSpec((db, bc), lambda j: (0, j + 2 * ncb)),
                  pl.BlockSpec((db, bc), lambda j: (0, j + 3 * ncb)),
                  pl.BlockSpec((conv_taps, bc), lambda j: (0, j)),
                  pl.BlockSpec((conv_taps, bc), lambda j: (0, j + ncb)),
                  pl.BlockSpec((1, bc), lambda j: (0, j)),
                  pl.BlockSpec((1, bc), lambda j: (0, j + ncb)),
                  pl.BlockSpec(memory_space=pl.ANY)],
        out_specs=pl.BlockSpec((e_rows, bc), lambda j: (rt // e_rows, j)),
        out_shape=jax.ShapeDtypeStruct((m_rows, dff), BF16),
        scratch_shapes=[pltpu.VMEM((8 + META_PAD, bc), F32), pltpu.VMEM((8 + META_PAD, bc), F32)],
        input_output_aliases={12: 0},
        compiler_params=_params("arbitrary"),
        name="conv_extra",
    )(up, up, up, up, sc2d, sc2d, sc2d, sc2d, cw, cw, cb, cb, act)

    bm_d = _pick_block(m_rows, 528, 16)
    bn_dn = _pick_block(d, 256, LANES)
    x2 = _matmul(act, w_d, m=m_rows, k=dff, n=d, bm=bm_d, bn=bn_dn, out_cols=d, out_bn=bn_dn, out_dtype=F32,
                 name="ffn_down", extra=(x1,), extra_specs=(pl.BlockSpec((bm_d, bn_dn), lambda i, j: (i, j)),),
                 epilogue=residual_epilogue)

    rb3 = _pick_block(rt, 512, 16)
    y_prompt = _norm_rows(x2, norm_final, row0=0, nrows=rt, rb=rb3, out_dtype=F32, name="norm_final_prompt")
    y_sample = _norm_rows(x2, norm_final, row0=r_samp, nrows=db, rb=db, out_dtype=F32, name="norm_final_sample")

    def with_meta(a, width):
        tok = a[:rt, :width].reshape(nb, s_len, width)
        meta = jnp.broadcast_to(a[r_meta:r_meta + nm, :width][None], (nb, nm, width))
        return jnp.concatenate([meta, tok], axis=1)[None]

    u_pool_tok = u1[:rt, :pw].reshape(nb, s_len, pw)
    up_tok = up[:rt].reshape(nb, s_len, dff2)
    return (
        y_prompt.reshape(nb, s_len, d),
        y_sample.reshape(db, 1, d),
        with_meta(ckv32, c_dim),
        with_meta(kpe32, r_dim),
        u_pool_tok[:, s_len - n_state:][None],
        up_tok[:, s_len - (conv_taps - 1):][None],
        ckv32[r_samp:r_samp + db].reshape(1, db, 1, c_dim),
        kpe32[r_samp:r_samp + db, :r_dim].reshape(1, db, 1, r_dim),
        jnp.concatenate([state_pool[0, :, 1:], u1[r_samp:r_samp + db, :pw][:, None]], axis=1)[None],
        jnp.concatenate([state_conv[0, :, 1:], up[r_samp:r_samp + db][:, None]], axis=1)[None],
    )
```

```python
import functools

import jax
import jax.numpy as jnp
from jax import lax
from jax.experimental import pallas as pl
from jax.experimental.pallas import tpu as pltpu

EPS = 1e-6
ROPE_BASE = 10000.0
POOL_WINDOWS = (2, 4, 8, 16)
LANES = 128
META_PAD = 128
POOL_ZERO_ROWS = 16
MASK_VALUE = -1e30
LOG2_E = 1.4426950408889634
VMEM_LIMIT_BYTES = 60 * 1024 * 1024

F32 = jnp.float32
BF16 = jnp.bfloat16


def _params(*sem):
    return pltpu.CompilerParams(dimension_semantics=sem, vmem_limit_bytes=VMEM_LIMIT_BYTES)


def _pick_block(n, cap, mult):
    best = None
    for d in range(mult, min(n, cap) + 1, mult):
        if n % d == 0:
            best = d
    assert best is not None, (n, cap, mult)
    return best


def _rms(x, g):
    return x * lax.rsqrt(jnp.mean(x * x, axis=-1, keepdims=True) + EPS) * g


def _norm_in_kernel(xt_ref, xe_ref, g_ref, x_ref, h_ref, *, n_tok_blocks):
    i = pl.program_id(0)

    def emit(src):
        x = src[...]
        x_ref[...] = x
        h_ref[...] = _rms(x, g_ref[...]).astype(h_ref.dtype)

    @pl.when(i < n_tok_blocks)
    def _():
        emit(xt_ref)

    @pl.when(i >= n_tok_blocks)
    def _():
        emit(xe_ref)


def _norm_in(x_tok, x_extra, g, rb):
    rt, d = x_tok.shape
    e = x_extra.shape[0]
    m = rt + e
    ntb, neb = rt // rb, e // rb
    return pl.pallas_call(
        functools.partial(_norm_in_kernel, n_tok_blocks=ntb),
        grid=(ntb + neb,),
        in_specs=[
            pl.BlockSpec((rb, d), lambda i: (jnp.minimum(i, ntb - 1), 0)),
            pl.BlockSpec((rb, d), lambda i: (jnp.maximum(i - ntb, 0), 0)),
            pl.BlockSpec((1, d), lambda i: (0, 0)),
        ],
        out_specs=[pl.BlockSpec((rb, d), lambda i: (i, 0)),
                   pl.BlockSpec((rb, d), lambda i: (i, 0))],
        out_shape=[jax.ShapeDtypeStruct((m, d), F32), jax.ShapeDtypeStruct((m, d), BF16)],
        compiler_params=_params("arbitrary"),
        name="norm_in",
    )(x_tok, x_extra, g.reshape(1, d))


def _norm_rows_kernel(x_ref, g_ref, o_ref):
    o_ref[...] = _rms(x_ref[...], g_ref[...]).astype(o_ref.dtype)


def _norm_rows(x, g, *, row0, nrows, rb, out_dtype, name):
    d = x.shape[1]
    assert row0 % rb == 0 and nrows % rb == 0
    b0 = row0 // rb
    return pl.pallas_call(
        _norm_rows_kernel,
        grid=(nrows // rb,),
        in_specs=[pl.BlockSpec((rb, d), lambda i: (b0 + i, 0)),
                  pl.BlockSpec((1, d), lambda i: (0, 0))],
        out_specs=pl.BlockSpec((rb, d), lambda i: (i, 0)),
        out_shape=jax.ShapeDtypeStruct((nrows, d), out_dtype),
        compiler_params=_params("arbitrary"),
        name=name,
    )(x, g.reshape(1, d))


def _mm_kernel(*refs, n_pro, n_extra, prologue, epilogue):
    x_ref, w_ref = refs[0], refs[1]
    pro_refs = refs[2:2 + n_pro]
    extra_refs = refs[2 + n_pro:2 + n_pro + n_extra]
    o_ref = refs[2 + n_pro + n_extra]
    if prologue is not None:
        xs_ref = refs[3 + n_pro + n_extra]

        @pl.when(pl.program_id(1) == 0)
        def _():
            xs_ref[...] = prologue(x_ref, *pro_refs).astype(xs_ref.dtype)

        xv = xs_ref[...]
    else:
        xv = x_ref[...]
    acc = jnp.dot(xv, w_ref[...].astype(BF16), preferred_element_type=F32)
    epilogue(acc, o_ref, *extra_refs)


def _store_epilogue(acc, o_ref):
    o_ref[...] = acc.astype(o_ref.dtype)


def _matmul(x, w, *, m, k, n, bm, bn, out_cols, out_bn, out_dtype, name,
            x_col_block=0, pro=(), pro_specs=(), prologue=None,
            extra=(), extra_specs=(), epilogue=_store_epilogue):
    assert m % bm == 0 and n % bn == 0
    scratch = [pltpu.VMEM((bm, k), BF16)] if prologue is not None else []
    kern = functools.partial(_mm_kernel, n_pro=len(pro), n_extra=len(extra),
                             prologue=prologue, epilogue=epilogue)
    return pl.pallas_call(
        kern,
        grid=(m // bm, n // bn),
        in_specs=[pl.BlockSpec((bm, k), lambda i, j: (i, x_col_block)),
                  pl.BlockSpec((k, bn), lambda i, j: (0, j)),
                  *pro_specs, *extra_specs],
        out_specs=pl.BlockSpec((bm, out_bn), lambda i, j: (i, j)),
        out_shape=jax.ShapeDtypeStruct((m, out_cols), out_dtype),
        scratch_shapes=scratch,
        compiler_params=_params("parallel", "arbitrary"),
        name=name,
    )(x, w, *pro, *extra)


def _rope_tile(v, ct, s1, s2, half):
    return v * ct + pltpu.roll(v, LANES - half, 1) * s1 + pltpu.roll(v, half, 1) * s2


def _kv_prep_kernel(ukv_ref, ukr_ref, g_ref, ct_ref, s1_ref, s2_ref, c32_ref, c16_ref, kpe_ref, *, half):
    c = _rms(ukv_ref[...], g_ref[...])
    c32_ref[...] = c
    c16_ref[...] = c.astype(BF16)
    kpe_ref[...] = _rope_tile(ukr_ref[...], ct_ref[...], s1_ref[...], s2_ref[...], half)


def _softmax_step(q, k, v, carry, mask=None):
    m_i, l_i, acc = carry
    s = lax.dot_general(q, k, (((1,), (1,)), ((), ())), preferred_element_type=F32)
    if mask is not None:
        s = jnp.where(mask, s, MASK_VALUE)
    m_new = jnp.maximum(m_i, jnp.max(s, axis=1, keepdims=True))
    alpha = jnp.exp2(m_i - m_new)
    p = jnp.exp2(s - m_new)
    l_new = alpha * l_i + jnp.sum(p, axis=1, keepdims=True)
    acc_new = alpha * acc + jnp.dot(p.astype(v.dtype), v, preferred_element_type=F32)
    return m_new, l_new, acc_new


def _zero_fill_extra_rows(o_ref, is_extra_step):
    @pl.when(is_extra_step)
    def _():
        o_ref[...] = jnp.zeros(o_ref.shape, o_ref.dtype)


def _attn_tok_kernel(q_ref, k_ref, v_ref, km_ref, vm_ref, o_ref, *, bq, n_meta, n_prompts):
    s_len, vdim = v_ref.shape
    _zero_fill_extra_rows(o_ref, pl.program_id(0) == n_prompts)

    @pl.when(pl.program_id(0) < n_prompts)
    def _():
        meta_mask = lax.broadcasted_iota(jnp.int32, (bq, km_ref.shape[0]), 1) < n_meta
        causal = (lax.broadcasted_iota(jnp.int32, (bq, bq), 1) <= lax.broadcasted_iota(jnp.int32, (bq, bq), 0))
        for qi in range(s_len // bq):
            rows = slice(qi * bq, (qi + 1) * bq)
            q = q_ref[rows, :]
            carry = (jnp.full((bq, 1), MASK_VALUE, F32), jnp.zeros((bq, 1), F32), jnp.zeros((bq, vdim), F32))
            carry = _softmax_step(q, km_ref[...], vm_ref[...], carry, meta_mask)
            for kj in range(qi):
                keys = slice(kj * bq, (kj + 1) * bq)
                carry = _softmax_step(q, k_ref[keys, :], v_ref[keys, :], carry)
            carry = _softmax_step(q, k_ref[rows, :], v_ref[rows, :], carry, causal)
            _, l_i, acc = carry
            o_ref[rows, :] = (acc / l_i).astype(o_ref.dtype)


def _attn_meta_kernel(q_ref, km_ref, vm_ref, o_any, o_ref, *, n_meta):
    del o_any
    n = q_ref.shape[0]
    row = lax.broadcasted_iota(jnp.int32, (n, n), 0)
    col = lax.broadcasted_iota(jnp.int32, (n, n), 1)
    mask = (col <= row) & (col < n_meta)
    carry = (jnp.full((n, 1), MASK_VALUE, F32), jnp.zeros((n, 1), F32),
             jnp.zeros((n, vm_ref.shape[1]), F32))
    _, l_i, acc = _softmax_step(q_ref[...], km_ref[...], vm_ref[...], carry, mask)
    o_ref[...] = (acc / l_i).astype(o_ref.dtype)


def _decode_kernel(pt_ref, ckv_hbm, kpe_hbm, q_ref, new_ref, o_ref,
                   ckv_buf, kpe_buf, sem, kc_ref, kp_ref, m_ref, l_ref, acc_ref,
                   *, pages, streams, page_size, c_dim, r_dim):
    b, j = pl.program_id(0), pl.program_id(1)
    n_b, n_j = pl.num_programs(0), pl.num_programs(1)
    step = b * n_j + j
    slot = step % 2
    per = pages // streams

    def page_copies(bb, jj, sl):
        copies = []
        for p in range(pages):
            page = pt_ref[bb, jj * pages + p]
            copies.append(pltpu.make_async_copy(ckv_hbm.at[page], ckv_buf.at[sl, p], sem.at[sl, 0]))
            copies.append(pltpu.make_async_copy(kpe_hbm.at[page], kpe_buf.at[sl, p], sem.at[sl, 1]))
        return copies

    @pl.when(step == 0)
    def _():
        for cp in page_copies(b, j, slot):
            cp.start()

    @pl.when(step + 1 < n_b * n_j)
    def _():
        wrap = j + 1 == n_j
        for cp in page_copies(jnp.where(wrap, b + 1, b), jnp.where(wrap, 0, j + 1), 1 - slot):
            cp.start()

    for cp in page_copies(b, j, slot):
        cp.wait()

    q = q_ref[0]
    q_lat, q_pe = q[:, :c_dim], q[:, c_dim:c_dim + r_dim]

    @pl.when(j == 0)
    def _():
        new = new_ref[0]
        s_self = jnp.sum(q.astype(F32) * new, axis=1, keepdims=True)
        m_ref[0] = s_self
        l_ref[0] = jnp.ones_like(s_self)
        acc_ref[0] = jnp.broadcast_to(new[:, :c_dim], acc_ref.shape[1:])
        for st in range(1, streams):
            m_ref[st] = jnp.full(m_ref.shape[1:], MASK_VALUE, F32)
            l_ref[st] = jnp.zeros(l_ref.shape[1:], F32)
            acc_ref[st] = jnp.zeros(acc_ref.shape[1:], F32)

    def scores(st):
        for p in range(per):
            keys = slice(p * page_size, (p + 1) * page_size)
            kc_ref[st, keys, :] = ckv_buf[slot, st * per + p].astype(kc_ref.dtype)
            kp_ref[st, :, keys] = kpe_buf[slot, st * per + p].astype(kp_ref.dtype)
        return (lax.dot_general(q_lat, kc_ref[st], (((1,), (1,)), ((), ())), preferred_element_type=F32)
                + jnp.dot(q_pe, kp_ref[st], preferred_element_type=F32))

    def softmax(st, s):
        m_old = m_ref[st]
        m_new = jnp.maximum(m_old, jnp.max(s, axis=1, keepdims=True))
        alpha = jnp.exp2(m_old - m_new)
        p_att = jnp.exp2(s - m_new)
        l_ref[st] = alpha * l_ref[st] + jnp.sum(p_att, axis=1, keepdims=True)
        m_ref[st] = m_new
        return alpha, p_att.astype(kc_ref.dtype)

    def values(st, alpha, p_att):
        acc_ref[st] = alpha * acc_ref[st] + jnp.dot(p_att, kc_ref[st], preferred_element_type=F32)

    pending_s, pending_p = None, None
    for st in range(streams):
        s = scores(st)
        if pending_p is not None:
            values(st - 2, *pending_p)
            pending_p = None
        if pending_s is not None:
            pending_p = softmax(st - 1, pending_s)
        pending_s = s
    if pending_p is not None:
        values(streams - 2, *pending_p)
    values(streams - 1, *softmax(streams - 1, pending_s))

    @pl.when(j == pl.num_programs(1) - 1)
    def _():
        m_all = m_ref[0]
        for st in range(1, streams):
            m_all = jnp.maximum(m_all, m_ref[st])
        l_all = jnp.zeros(l_ref.shape[1:], F32)
        acc_all = jnp.zeros(acc_ref.shape[1:], F32)
        for st in range(streams):
            w = jnp.exp2(m_ref[st] - m_all)
            l_all = l_all + w * l_ref[st]
            acc_all = acc_all + w * acc_ref[st]
        o_ref[0] = (acc_all / l_all).astype(o_ref.dtype)


def _decode_attention(page_table, cache_ckv, cache_kpe_t, q_cat, new_cat, *, pages, streams):
    db, n_pages = page_table.shape
    _, page_size, c_dim = cache_ckv.shape
    r_dim = cache_kpe_t.shape[1]
    n_heads, qk_dim = q_cat.shape[1], q_cat.shape[2]
    assert n_pages % pages == 0 and pages % streams == 0
    per = pages // streams

    grid_spec = pltpu.PrefetchScalarGridSpec(
        num_scalar_prefetch=1,
        grid=(db, n_pages // pages),
        in_specs=[pl.BlockSpec(memory_space=pl.ANY),
                  pl.BlockSpec(memory_space=pl.ANY),
                  pl.BlockSpec((1, n_heads, qk_dim), lambda b, j, pt: (b, 0, 0)),
                  pl.BlockSpec((1, 1, qk_dim), lambda b, j, pt: (b, 0, 0))],
        out_specs=pl.BlockSpec((1, n_heads, c_dim), lambda b, j, pt: (b, 0, 0)),
        scratch_shapes=[pltpu.VMEM((2, pages, page_size, c_dim), cache_ckv.dtype),
                        pltpu.VMEM((2, pages, r_dim, page_size), cache_kpe_t.dtype),
                        pltpu.SemaphoreType.DMA((2, 2)),
                        pltpu.VMEM((streams, per * page_size, c_dim), BF16),
                        pltpu.VMEM((streams, r_dim, per * page_size), BF16),
                        pltpu.VMEM((streams, n_heads, 1), F32),
                        pltpu.VMEM((streams, n_heads, 1), F32),
                        pltpu.VMEM((streams, n_heads, c_dim), F32)],
    )
    kern = functools.partial(_decode_kernel, pages=pages, streams=streams, page_size=page_size,
                             c_dim=c_dim, r_dim=r_dim)
    return pl.pallas_call(
        kern,
        grid_spec=grid_spec,
        out_shape=jax.ShapeDtypeStruct((db, n_heads, c_dim), F32),
        compiler_params=_params("arbitrary", "arbitrary"),
        name="decode_attention",
    )(page_table, cache_ckv, cache_kpe_t, q_cat, new_cat)


def _qlat_kernel(q_ref, w_ref, o_ref):
    o_ref[0] = jnp.dot(q_ref[...], w_ref[0], preferred_element_type=F32)


def _ouv_kernel(x_ref, w_ref, o_any, o_ref):
    del o_any
    o_ref[...] = jnp.dot(x_ref[0], w_ref[...], preferred_element_type=F32).astype(o_ref.dtype)


def _pool_tok_kernel(u_ref, um_ref, w_ref, sc_ref, o_ref, z_ref, t_ref, *, n_meta, n_prompts):
    g = pl.program_id(1)
    is_prompt = pl.program_id(0) < n_prompts
    _zero_fill_extra_rows(o_ref, pl.program_id(0) == n_prompts)
    s_len = u_ref.shape[0]
    zp = POOL_ZERO_ROWS
    end = zp + n_meta + s_len
    z_ref[:zp, :] = jnp.zeros((zp, z_ref.shape[1]), F32)
    t_ref[:zp, :] = jnp.zeros((zp, t_ref.shape[1]), F32)
    z_ref[zp:zp + n_meta, :] = um_ref[...]
    z_ref[zp + n_meta:, :] = u_ref[...]

    def window_sum(win):
        src, dst = z_ref, t_ref
        d = 1
        while d < win:
            dst[zp:end, :] = src[zp:end, :] + src[zp - d:end - d, :]
            src, dst = dst, src
            d *= 2
        return src[zp + n_meta:end, :]

    for gi, win in enumerate(POOL_WINDOWS):
        @pl.when((g == gi) & is_prompt)
        def _(win=win):
            mix = window_sum(win) / float(win) - u_ref[...]
            y = jnp.dot(mix.astype(BF16), w_ref[0], preferred_element_type=F32)
            o_ref[...] = (y * sc_ref[...]).astype(o_ref.dtype)


def _pool_extra_kernel(st_ref, us_ref, um_ref, w_ref, sc_ref, o_any, o_ref, acc_ref, zz_ref,
                       *, n_state, pg, n_meta_pad):
    del o_any
    j = pl.program_id(0)

    @pl.when(j == 0)
    def _():
        acc_ref[...] = jnp.zeros_like(acc_ref)

    for gi, win in enumerate(POOL_WINDOWS):
        cols = slice(gi * pg, (gi + 1) * pg)

        @pl.when(n_state - j <= win - 1)
        def _(cols=cols):
            acc_ref[:, cols] += st_ref[0, :, cols]

    @pl.when(j == n_state - 1)
    def _():
        zpad = zz_ref.shape[0] - n_meta_pad
        zz_ref[:zpad, :] = jnp.zeros((zpad, zz_ref.shape[1]), F32)
        zz_ref[zpad:, :] = um_ref[...]
        t = lax.broadcasted_iota(jnp.int32, (n_meta_pad, 1), 0)
        for gi, win in enumerate(POOL_WINDOWS):
            cols = slice(gi * pg, (gi + 1) * pg)
            us = us_ref[:, cols]
            mix_s = (acc_ref[:, cols] + us) / float(win) - us
            wsum = zz_ref[zpad:, cols]
            for kk in range(1, win):
                wsum = wsum + zz_ref[zpad - kk:zpad - kk + n_meta_pad, cols]
            cnt = jnp.minimum(t + 1, win).astype(F32)
            mix_m = wsum / cnt - um_ref[:, cols]
            w = w_ref[gi]
            sc = sc_ref[:, cols]
            o_ref[:n_meta_pad, cols] = (jnp.dot(mix_m.astype(BF16), w, preferred_element_type=F32)
                                        * sc).astype(o_ref.dtype)
            o_ref[n_meta_pad:, cols] = (jnp.dot(mix_s.astype(BF16), w, preferred_element_type=F32)
                                        * sc).astype(o_ref.dtype)


def _conv3(x0, x1, x2, w_ref, b_ref):
    return b_ref[...] + x0 * w_ref[0:1, :] + x1 * w_ref[1:2, :] + x2 * w_ref[2:3, :]


def _conv_tok_kernel(ug_ref, uv_ref, mg_ref, mv_ref, wg_ref, wv_ref, bg_ref, bv_ref, o_ref,
                     zg_ref, zv_ref, *, n_meta, n_prompts):
    s_len = ug_ref.shape[0]
    _zero_fill_extra_rows(o_ref, pl.program_id(0) == n_prompts)

    def conv(u_ref, m_ref, z_ref, w_ref, b_ref):
        z_ref[:n_meta, :] = m_ref[...]
        z_ref[n_meta:, :] = u_ref[...]
        return _conv3(z_ref[n_meta - 2:n_meta - 2 + s_len, :], z_ref[n_meta - 1:n_meta - 1 + s_len, :],
                      u_ref[...], w_ref, b_ref)

    @pl.when(pl.program_id(0) < n_prompts)
    def _():
        gate = conv(ug_ref, mg_ref, zg_ref, wg_ref, bg_ref)
        val = conv(uv_ref, mv_ref, zv_ref, wv_ref, bv_ref)
        o_ref[...] = (jax.nn.silu(gate) * val).astype(o_ref.dtype)


def _conv_extra_kernel(mg_ref, mv_ref, sg_ref, sv_ref, p0g_ref, p0v_ref, p1g_ref, p1v_ref,
                       wg_ref, wv_ref, bg_ref, bv_ref, o_any, o_ref, zg_ref, zv_ref, *, n_meta_pad):
    del o_any
    zpad = zg_ref.shape[0] - n_meta_pad

    def conv_meta(m_ref, z_ref, w_ref, b_ref):
        z_ref[:zpad, :] = jnp.zeros((zpad, z_ref.shape[1]), F32)
        z_ref[zpad:, :] = m_ref[...]
        return _conv3(z_ref[zpad - 2:zpad - 2 + n_meta_pad, :], z_ref[zpad - 1:zpad - 1 + n_meta_pad, :],
                      m_ref[...], w_ref, b_ref)

    gate_m = conv_meta(mg_ref, zg_ref, wg_ref, bg_ref)
    val_m = conv_meta(mv_ref, zv_ref, wv_ref, bv_ref)
    o_ref[:n_meta_pad, :] = (jax.nn.silu(gate_m) * val_m).astype(o_ref.dtype)
    gate_s = _conv3(p0g_ref[...], p1g_ref[...], sg_ref[...], wg_ref, bg_ref)
    val_s = _conv3(p0v_ref[...], p1v_ref[...], sv_ref[...], wv_ref, bv_ref)
    o_ref[n_meta_pad:, :] = (jax.nn.silu(gate_s) * val_s).astype(o_ref.dtype)


def kernel(x_prompt, x_sample, cache_ckv, cache_kpe, state_pool, state_conv, page_table, meta_tokens,
           norm_mix, w_in, g_qnorm, w_uq, g_kvnorm, w_uk, w_uv, w_attn_branch, w_pool_grp, pool_scale,
           w_pool_branch, w_out, norm_ffn, w_up, conv_w, conv_b, w_down, norm_final):
    nb, s_len, d = x_prompt.shape
    db, dec_t, _ = x_sample.shape
    depth = w_in.shape[0]
    assert depth == 1 and dec_t == 1
    nm = meta_tokens.shape[0]
    _, c_dim, n_heads, d_nope = w_uk.shape
    v_dim = w_uv.shape[3]
    r_dim = cache_kpe.shape[3]
    half = r_dim // 2
    q_lora = g_qnorm.shape[1]
    pw = pool_scale.shape[1]
    n_grp = len(POOL_WINDOWS)
    pg = pw // n_grp
    n_state = state_pool.shape[2]
    conv_taps = conv_w.shape[1]
    dff2 = w_up.shape[2]
    dff = dff2 // 2
    page_size = cache_ckv.shape[2]
    n_pages = page_table.shape[1]
    past = n_pages * page_size
    scale = float(d_nope + r_dim) ** -0.5 * LOG2_E
    hd = 2 * LANES
    assert d_nope == LANES and v_dim == LANES and r_dim <= LANES and conv_taps == 3
    assert nm >= max(POOL_WINDOWS) - 1 and nm % 16 == 0 and nm <= META_PAD
    assert n_state == max(POOL_WINDOWS) - 1

    rt = nb * s_len
    r_meta = rt
    r_samp = rt + META_PAD
    e_rows = META_PAD + db
    m_rows = rt + e_rows
    assert rt % META_PAD == 0 and r_samp % db == 0 and db % 16 == 0 and r_meta % nm == 0

    off_q, off_kv, off_kr, off_gp = pw, pw + q_lora, pw + q_lora + c_dim, pw + q_lora + c_dim + r_dim
    assert off_q % q_lora == 0 and off_kv % c_dim == 0 and off_kr % LANES == 0 and pw % pg == 0

    n_a = off_kr + LANES
    n_a_pad = -(-n_a // 256) * 256
    w_a = jnp.concatenate([w_in[0, :, :off_gp], jnp.zeros((d, n_a_pad - off_gp), F32)], axis=1).astype(BF16)
    w_g = w_in[0, :, off_gp:].astype(BF16)
    wq = w_uq[0].reshape(q_lora, n_heads, d_nope + r_dim)
    wq = jnp.concatenate([wq, jnp.zeros((q_lora, n_heads, hd - d_nope - r_dim), F32)], axis=2)
    wq = wq.reshape(q_lora, n_heads * hd).astype(BF16)
    w_uk2 = w_uk[0].reshape(c_dim, n_heads * d_nope).astype(BF16)
    w_ukt = jnp.transpose(w_uk[0], (1, 2, 0)).astype(BF16)
    w_uv2 = w_uv[0].reshape(c_dim, n_heads * v_dim).astype(BF16)
    w_pg = w_pool_grp[0].astype(BF16)
    w_d = w_down[0].astype(BF16)
    w_ab, w_pb, w_o, w_u = w_attn_branch[0], w_pool_branch[0], w_out[0], w_up[0]

    pos = jnp.concatenate([
        jnp.tile(nm + jnp.arange(s_len), nb),
        jnp.arange(nm), jnp.zeros((META_PAD - nm,), jnp.int32),
        jnp.full((db,), past, jnp.int32)])
    inv = ROPE_BASE ** (-jnp.arange(0, r_dim, 2, dtype=F32) / r_dim)
    ang = pos.astype(F32)[:, None] * inv[None, :]
    cos, sin = jnp.cos(ang), jnp.sin(ang)
    zl = jnp.zeros((m_rows, LANES - 2 * half), F32)
    zh = jnp.zeros((m_rows, half), F32)
    tab_c = jnp.concatenate([cos, cos, zl], axis=1)
    tab_s1 = jnp.concatenate([-sin, zh, zl], axis=1)
    tab_s2 = jnp.concatenate([zh, sin, zl], axis=1)

    bm = _pick_block(m_rows, 1100, 16)
    row_tab_spec = pl.BlockSpec((bm, LANES), lambda i, j: (i, 0))

    x_extra = jnp.concatenate([meta_tokens.astype(F32), jnp.zeros((META_PAD - nm, d), F32),
                               x_sample.reshape(db, d)], axis=0)
    rb = _pick_block(e_rows, 256, 16)
    assert rt % rb == 0
    x_all, h = _norm_in(x_prompt.reshape(rt, d), x_extra, norm_mix[0], rb)

    bn_a = _pick_block(n_a_pad, 768, LANES)
    u1 = _matmul(h, w_a, m=m_rows, k=d, n=n_a_pad, bm=bm, bn=bn_a, out_cols=n_a_pad, out_bn=bn_a,
                 out_dtype=F32, name="in_proj")

    def gate_epilogue(acc, o_ref):
        o_ref[...] = jax.nn.sigmoid(acc).astype(o_ref.dtype)

    bn_d = _pick_block(d, 512, LANES)
    gates = _matmul(h, w_g, m=m_rows, k=d, n=2 * d, bm=bm, bn=bn_d, out_cols=2 * d, out_bn=bn_d,
                    out_dtype=BF16, name="gate_proj", epilogue=gate_epilogue)

    def q_prologue(x_ref, g_ref):
        return _rms(x_ref[...], g_ref[...])

    bn_q = _pick_block(n_heads * hd, 512, hd)

    def q_epilogue(acc, o_ref, ct_ref, s1_ref, s2_ref):
        acc = acc * scale
        for cb in range(bn_q // LANES):
            cols = slice(cb * LANES, (cb + 1) * LANES)
            v = acc[:, cols]
            if cb % 2 == 1:
                v = _rope_tile(v, ct_ref[...], s1_ref[...], s2_ref[...], half)
            o_ref[:, cols] = v.astype(o_ref.dtype)

    q_all = _matmul(u1, wq, m=m_rows, k=q_lora, n=n_heads * hd, bm=bm, bn=bn_q,
                    out_cols=n_heads * hd, out_bn=bn_q, out_dtype=BF16, name="q_proj",
                    x_col_block=off_q // q_lora,
                    pro=(g_qnorm[0].reshape(1, q_lora),), pro_specs=(pl.BlockSpec((1, q_lora), lambda i, j: (0, 0)),),
                    prologue=q_prologue,
                    extra=(tab_c, tab_s1, tab_s2), extra_specs=(row_tab_spec,) * 3, epilogue=q_epilogue)

    tab1 = pl.BlockSpec((bm, LANES), lambda i: (i, 0))
    ckv32, ckv16, kpe32 = pl.pallas_call(
        functools.partial(_kv_prep_kernel, half=half),
        grid=(m_rows // bm,),
        in_specs=[pl.BlockSpec((bm, c_dim), lambda i: (i, off_kv // c_dim)),
                  pl.BlockSpec((bm, LANES), lambda i: (i, off_kr // LANES)),
                  pl.BlockSpec((1, c_dim), lambda i: (0, 0)),
                  tab1, tab1, tab1],
        out_specs=[pl.BlockSpec((bm, c_dim), lambda i: (i, 0)),
                   pl.BlockSpec((bm, c_dim), lambda i: (i, 0)),
                   pl.BlockSpec((bm, LANES), lambda i: (i, 0))],
        out_shape=[jax.ShapeDtypeStruct((m_rows, c_dim), F32),
                   jax.ShapeDtypeStruct((m_rows, c_dim), BF16),
                   jax.ShapeDtypeStruct((m_rows, LANES), F32)],
        compiler_params=_params("arbitrary"),
        name="kv_prep",
    )(u1, u1, g_kvnorm[0].reshape(1, c_dim), tab_c, tab_s1, tab_s2)

    hps = _pick_block(n_heads, 4, 1)

    def k_epilogue(acc, o_ref, kpe_ref):
        kr = kpe_ref[...].astype(o_ref.dtype)
        for hh in range(hps):
            o_ref[:, hh * hd:hh * hd + d_nope] = acc[:, hh * d_nope:(hh + 1) * d_nope].astype(o_ref.dtype)
            o_ref[:, hh * hd + d_nope:(hh + 1) * hd] = kr

    k_all = _matmul(ckv16, w_uk2, m=m_rows, k=c_dim, n=n_heads * d_nope, bm=bm, bn=hps * d_nope,
                    out_cols=n_heads * hd, out_bn=hps * hd, out_dtype=BF16, name="k_up",
                    extra=(kpe32,), extra_specs=(row_tab_spec,), epilogue=k_epilogue)
    bn_v = _pick_block(n_heads * v_dim, 512, LANES)
    v_all = _matmul(ckv16, w_uv2, m=m_rows, k=c_dim, n=n_heads * v_dim, bm=bm, bn=bn_v,
                    out_cols=n_heads * v_dim, out_bn=bn_v, out_dtype=BF16, name="v_up")

    bq = _pick_block(s_len, 512, 128)
    mb = r_meta // META_PAD
    assert e_rows <= s_len

    def prompt_block(b, col, col_off=0):
        return jnp.minimum(b, nb - 1), jnp.where(b < nb, col, 0) + col_off
    o_all = pl.pallas_call(
        functools.partial(_attn_tok_kernel, bq=bq, n_meta=nm, n_prompts=nb),
        grid=(nb + 1, n_heads),
        in_specs=[pl.BlockSpec((s_len, hd), prompt_block),
                  pl.BlockSpec((s_len, hd), prompt_block),
                  pl.BlockSpec((s_len, v_dim), prompt_block),
                  pl.BlockSpec((META_PAD, hd), lambda b, hh: (mb, hh)),
                  pl.BlockSpec((META_PAD, v_dim), lambda b, hh: (mb, hh))],
        out_specs=pl.BlockSpec((s_len, v_dim), lambda b, hh: (b, hh)),
        out_shape=jax.ShapeDtypeStruct((m_rows, n_heads * v_dim), BF16),
        compiler_params=_params("parallel", "arbitrary"),
        name="attn_prompt",
    )(q_all, k_all, v_all, k_all, v_all)

    o_all = pl.pallas_call(
        functools.partial(_attn_meta_kernel, n_meta=nm),
        grid=(n_heads,),
        in_specs=[pl.BlockSpec((META_PAD, hd), lambda hh: (mb, hh)),
                  pl.BlockSpec((META_PAD, hd), lambda hh: (mb, hh)),
                  pl.BlockSpec((META_PAD, v_dim), lambda hh: (mb, hh)),
                  pl.BlockSpec(memory_space=pl.ANY)],
        out_specs=pl.BlockSpec((META_PAD, v_dim), lambda hh: (mb, hh)),
        out_shape=jax.ShapeDtypeStruct((m_rows, n_heads * v_dim), BF16),
        input_output_aliases={3: 0},
        compiler_params=_params("arbitrary"),
        name="attn_meta",
    )(q_all, k_all, v_all, o_all)

    sb = r_samp // db
    q_lat = pl.pallas_call(
        _qlat_kernel,
        grid=(n_heads,),
        in_specs=[pl.BlockSpec((db, d_nope), lambda hh: (sb, hh * (hd // d_nope))),
                  pl.BlockSpec((1, d_nope, c_dim), lambda hh: (hh, 0, 0))],
        out_specs=pl.BlockSpec((1, db, c_dim), lambda hh: (hh, 0, 0)),
        out_shape=jax.ShapeDtypeStruct((n_heads, db, c_dim), F32),
        compiler_params=_params("arbitrary"),
        name="q_absorb",
    )(q_all, w_ukt)
    qk_dim = -(-(c_dim + r_dim) // LANES) * LANES
    q_samp = q_all[r_samp:r_samp + db].reshape(db, n_heads, hd)
    q_cat = jnp.concatenate([jnp.transpose(q_lat, (1, 0, 2)).astype(BF16),
                             q_samp[:, :, d_nope:d_nope + r_dim],
                             jnp.zeros((db, n_heads, qk_dim - c_dim - r_dim), BF16)], axis=2)
    new_cat = jnp.concatenate([ckv32[r_samp:r_samp + db], kpe32[r_samp:r_samp + db, :r_dim],
                               jnp.zeros((db, qk_dim - c_dim - r_dim), F32)], axis=1).reshape(db, 1, qk_dim)
    pages = _pick_block(n_pages, 32, 1)
    streams = _pick_block(pages, 4, 1)
    cache_kpe_t = jnp.swapaxes(cache_kpe[0], 1, 2)
    o_lat = _decode_attention(page_table, cache_ckv[0], cache_kpe_t, q_cat, new_cat, pages=pages, streams=streams)
    o_lat_t = jnp.transpose(o_lat, (1, 0, 2)).astype(BF16)
    o_all = pl.pallas_call(
        _ouv_kernel,
        grid=(n_heads,),
        in_specs=[pl.BlockSpec((1, db, c_dim), lambda hh: (hh, 0, 0)),
                  pl.BlockSpec((c_dim, v_dim), lambda hh: (0, hh)),
                  pl.BlockSpec(memory_space=pl.ANY)],
        out_specs=pl.BlockSpec((db, v_dim), lambda hh: (sb, hh)),
        out_shape=jax.ShapeDtypeStruct((m_rows, n_heads * v_dim), BF16),
        input_output_aliases={2: 0},
        compiler_params=_params("arbitrary"),
        name="o_up_decode",
    )(o_lat_t, w_uv2, o_all)

    scale_row = pool_scale[0].reshape(1, pw)
    pm = pl.pallas_call(
        functools.partial(_pool_tok_kernel, n_meta=nm, n_prompts=nb),
        grid=(nb + 1, n_grp),
        in_specs=[pl.BlockSpec((s_len, pg), prompt_block),
                  pl.BlockSpec((nm, pg), lambda b, g: (r_meta // nm, g)),
                  pl.BlockSpec((1, pg, pg), lambda b, g: (g, 0, 0)),
                  pl.BlockSpec((1, pg), lambda b, g: (0, g))],
        out_specs=pl.BlockSpec((s_len, pg), lambda b, g: (b, g)),
        out_shape=jax.ShapeDtypeStruct((m_rows, pw), BF16),
        scratch_shapes=[pltpu.VMEM((POOL_ZERO_ROWS + nm + s_len, pg), F32),
                        pltpu.VMEM((POOL_ZERO_ROWS + nm + s_len, pg), F32)],
        compiler_params=_params("parallel", "arbitrary"),
        name="pool_prompt",
    )(u1, u1, w_pg, scale_row)
    st_rows = jnp.transpose(state_pool[0], (1, 0, 2))
    zrows = POOL_ZERO_ROWS
    pm = pl.pallas_call(
        functools.partial(_pool_extra_kernel, n_state=n_state, pg=pg, n_meta_pad=META_PAD),
        grid=(n_state,),
        in_specs=[pl.BlockSpec((1, db, pw), lambda j: (j, 0, 0)),
                  pl.BlockSpec((db, pw), lambda j: (sb, 0)),
                  pl.BlockSpec((META_PAD, pw), lambda j: (mb, 0)),
                  pl.BlockSpec((n_grp, pg, pg), lambda j: (0, 0, 0)),
                  pl.BlockSpec((1, pw), lambda j: (0, 0)),
                  pl.BlockSpec(memory_space=pl.ANY)],
        out_specs=pl.BlockSpec((e_rows, pw), lambda j: (rt // e_rows, 0)),
        out_shape=jax.ShapeDtypeStruct((m_rows, pw), BF16),
        scratch_shapes=[pltpu.VMEM((db, pw), F32), pltpu.VMEM((zrows + META_PAD, pw), F32)],
        input_output_aliases={5: 0},
        compiler_params=_params("arbitrary"),
        name="pool_extra",
    )(st_rows, u1, u1, w_pg, scale_row, pm)

    def pool_branch_epilogue(acc, o_ref, g_ref):
        o_ref[...] = (g_ref[...].astype(F32) * acc).astype(o_ref.dtype)

    tile_spec = pl.BlockSpec((bm, bn_d), lambda i, j: (i, j))
    t1 = _matmul(pm, w_pb, m=m_rows, k=pw, n=d, bm=bm, bn=bn_d, out_cols=d, out_bn=bn_d, out_dtype=BF16,
                 name="pool_branch", extra=(gates,), extra_specs=(tile_spec,), epilogue=pool_branch_epilogue)

    def attn_branch_epilogue(acc, o_ref, g_ref, t_ref):
        o_ref[...] = (t_ref[...].astype(F32) + g_ref[...].astype(F32) * acc).astype(o_ref.dtype)

    mixed = _matmul(o_all, w_ab, m=m_rows, k=n_heads * v_dim, n=d, bm=bm, bn=bn_d, out_cols=d, out_bn=bn_d,
                    out_dtype=BF16, name="attn_branch", extra=(gates, t1),
                    extra_specs=(pl.BlockSpec((bm, bn_d), lambda i, j: (i, j + d // bn_d)), tile_spec),
                    epilogue=attn_branch_epilogue)

    def residual_epilogue(acc, o_ref, r_ref):
        o_ref[...] = r_ref[...] + acc

    x1 = _matmul(mixed, w_o, m=m_rows, k=d, n=d, bm=bm, bn=bn_d, out_cols=d, out_bn=bn_d, out_dtype=F32,
                 name="out_proj", extra=(x_all,), extra_specs=(tile_spec,), epilogue=residual_epilogue)

    rb2 = _pick_block(m_rows, 512, 16)
    h2 = _norm_rows(x1, norm_ffn[0], row0=0, nrows=m_rows, rb=rb2, out_dtype=BF16, name="norm_ffn")
    bn_u = _pick_block(dff2, 512, LANES)
    up = _matmul(h2, w_u, m=m_rows, k=d, n=dff2, bm=bm, bn=bn_u, out_cols=dff2, out_bn=bn_u, out_dtype=F32,
                 name="ffn_up")

    bc = _pick_block(dff, 256, LANES)
    ncb = dff // bc
    cw, cb = conv_w[0], conv_b[0].reshape(1, dff2)
    act = pl.pallas_call(
        functools.partial(_conv_tok_kernel, n_meta=nm, n_prompts=nb),
        grid=(nb + 1, ncb),
        in_specs=[pl.BlockSpec((s_len, bc), prompt_block),
                  pl.BlockSpec((s_len, bc), lambda b, j: prompt_block(b, j, ncb)),
                  pl.BlockSpec((nm, bc), lambda b, j: (r_meta // nm, j)),
                  pl.BlockSpec((nm, bc), lambda b, j: (r_meta // nm, j + ncb)),
                  pl.BlockSpec((conv_taps, bc), lambda b, j: (0, j)),
                  pl.BlockSpec((conv_taps, bc), lambda b, j: (0, j + ncb)),
                  pl.BlockSpec((1, bc), lambda b, j: (0, j)),
                  pl.BlockSpec((1, bc), lambda b, j: (0, j + ncb))],
        out_specs=pl.BlockSpec((s_len, bc), lambda b, j: (b, j)),
        out_shape=jax.ShapeDtypeStruct((m_rows, dff), BF16),
        scratch_shapes=[pltpu.VMEM((nm + s_len, bc), F32), pltpu.VMEM((nm + s_len, bc), F32)],
        compiler_params=_params("parallel", "arbitrary"),
        name="conv_prompt",
    )(up, up, up, up, cw, cw, cb, cb)
    sc2d = state_conv[0].reshape(db, (conv_taps - 1) * dff2)
    act = pl.pallas_call(
        functools.partial(_conv_extra_kernel, n_meta_pad=META_PAD),
        grid=(ncb,),
        in_specs=[pl.BlockSpec((META_PAD, bc), lambda j: (mb, j)),
                  pl.BlockSpec((META_PAD, bc), lambda j: (mb, j + ncb)),
                  pl.BlockSpec((db, bc), lambda j: (sb, j)),
                  pl.BlockSpec((db, bc), lambda j: (sb, j + ncb)),
                  pl.BlockSpec((db, bc), lambda j: (0, j)),
                  pl.BlockSpec((db, bc), lambda j: (0, j + ncb)),
                  pl.BlockSpec((db, bc), lambda j: (0, j + 2 * ncb)),
                  pl.BlockSpec((db, bc), lambda j: (0, j + 3 * ncb)),
                  pl.BlockSpec((conv_taps, bc), lambda j: (0, j)),
                  pl.BlockSpec((conv_taps, bc), lambda j: (0, j + ncb)),
                  pl.BlockSpec((1, bc), lambda j: (0, j)),
                  pl.BlockSpec((1, bc), lambda j: (0, j + ncb)),
                  pl.BlockSpec(memory_space=pl.ANY)],
        out_specs=pl.BlockSpec((e_rows, bc), lambda j: (rt // e_rows, j)),
        out_shape=jax.ShapeDtypeStruct((m_rows, dff), BF16),
        scratch_shapes=[pltpu.VMEM((8 + META_PAD, bc), F32), pltpu.VMEM((8 + META_PAD, bc), F32)],
        input_output_aliases={12: 0},
        compiler_params=_params("arbitrary"),
        name="conv_extra",
    )(up, up, up, up, sc2d, sc2d, sc2d, sc2d, cw, cw, cb, cb, act)

    bm_d = _pick_block(m_rows, 528, 16)
    bn_dn = _pick_block(d, 256, LANES)
    x2 = _matmul(act, w_d, m=m_rows, k=dff, n=d, bm=bm_d, bn=bn_dn, out_cols=d, out_bn=bn_dn, out_dtype=F32,
                 name="ffn_down", extra=(x1,), extra_specs=(pl.BlockSpec((bm_d, bn_dn), lambda i, j: (i, j)),),
                 epilogue=residual_epilogue)

    rb3 = _pick_block(rt, 512, 16)
    y_prompt = _norm_rows(x2, norm_final, row0=0, nrows=rt, rb=rb3, out_dtype=F32, name="norm_final_prompt")
    y_sample = _norm_rows(x2, norm_final, row0=r_samp, nrows=db, rb=db, out_dtype=F32, name="norm_final_sample")

    def with_meta(a, width):
        tok = a[:rt, :width].reshape(nb, s_len, width)
        meta = jnp.broadcast_to(a[r_meta:r_meta + nm, :width][None], (nb, nm, width))
        return jnp.concatenate([meta, tok], axis=1)[None]

    def prompt_tail(a, n_tail, width):
        return jnp.stack([a[(b + 1) * s_len - n_tail:(b + 1) * s_len, :width] for b in range(nb)])[None]

    return (
        y_prompt.reshape(nb, s_len, d),
        y_sample.reshape(db, 1, d),
        with_meta(ckv32, c_dim),
        with_meta(kpe32, r_dim),
        prompt_tail(u1, n_state, pw),
        prompt_tail(up, conv_taps - 1, dff2),
        ckv32[r_samp:r_samp + db].reshape(1, db, 1, c_dim),
        kpe32[r_samp:r_samp + db, :r_dim].reshape(1, db, 1, r_dim),
        jnp.concatenate([state_pool[0, :, 1:], u1[r_samp:r_samp + db, :pw][:, None]], axis=1)[None],
        jnp.concatenate([state_conv[0, :, 1:], up[r_samp:r_samp + db][:, None]], axis=1)[None],
    )
```

```python
import functools

import jax
import jax.numpy as jnp
from jax import lax
from jax.experimental import pallas as pl
from jax.experimental.pallas import tpu as pltpu

EPS = 1e-6
ROPE_BASE = 10000.0
POOL_WINDOWS = (2, 4, 8, 16)
LANES = 128
META_PAD = 128
POOL_ZERO_ROWS = 16
MASK_VALUE = -1e30
LOG2_E = 1.4426950408889634
N_K_CHUNKS = 16
N_ROW_PIECES = 8
VMEM_LIMIT_BYTES = 60 * 1024 * 1024

F32 = jnp.float32
BF16 = jnp.bfloat16


def _params(*sem):
    return pltpu.CompilerParams(dimension_semantics=sem, vmem_limit_bytes=VMEM_LIMIT_BYTES)


def _pick_block(n, cap, mult):
    best = None
    for d in range(mult, min(n, cap) + 1, mult):
        if n % d == 0:
            best = d
    assert best is not None, (n, cap, mult)
    return best


def _rms(x, g):
    return x * lax.rsqrt(jnp.mean(x * x, axis=-1, keepdims=True) + EPS) * g


def _norm_in_kernel(xt_ref, xe_ref, g_ref, x_ref, h_ref, *, n_tok_blocks):
    i = pl.program_id(0)

    def emit(src):
        x = src[...]
        x_ref[...] = x
        h_ref[...] = _rms(x, g_ref[...]).astype(h_ref.dtype)

    @pl.when(i < n_tok_blocks)
    def _():
        emit(xt_ref)

    @pl.when(i >= n_tok_blocks)
    def _():
        emit(xe_ref)


def _norm_in(x_tok, x_extra, g, rb):
    rt, d = x_tok.shape
    e = x_extra.shape[0]
    m = rt + e
    ntb, neb = rt // rb, e // rb
    return pl.pallas_call(
        functools.partial(_norm_in_kernel, n_tok_blocks=ntb),
        grid=(ntb + neb,),
        in_specs=[
            pl.BlockSpec((rb, d), lambda i: (jnp.minimum(i, ntb - 1), 0)),
            pl.BlockSpec((rb, d), lambda i: (jnp.maximum(i - ntb, 0), 0)),
            pl.BlockSpec((1, d), lambda i: (0, 0)),
        ],
        out_specs=[pl.BlockSpec((rb, d), lambda i: (i, 0)),
                   pl.BlockSpec((rb, d), lambda i: (i, 0))],
        out_shape=[jax.ShapeDtypeStruct((m, d), F32), jax.ShapeDtypeStruct((m, d), BF16)],
        compiler_params=_params("arbitrary"),
        name="norm_in",
    )(x_tok, x_extra, g.reshape(1, d))


def _norm_rows_kernel(x_ref, g_ref, o_ref):
    o_ref[...] = _rms(x_ref[...], g_ref[...]).astype(o_ref.dtype)


def _norm_rows(x, g, *, row0, nrows, rb, out_dtype, name):
    d = x.shape[1]
    assert row0 % rb == 0 and nrows % rb == 0
    b0 = row0 // rb
    return pl.pallas_call(
        _norm_rows_kernel,
        grid=(nrows // rb,),
        in_specs=[pl.BlockSpec((rb, d), lambda i: (b0 + i, 0)),
                  pl.BlockSpec((1, d), lambda i: (0, 0))],
        out_specs=pl.BlockSpec((rb, d), lambda i: (i, 0)),
        out_shape=jax.ShapeDtypeStruct((nrows, d), out_dtype),
        compiler_params=_params("arbitrary"),
        name=name,
    )(x, g.reshape(1, d))


def _mm_kernel(*refs, n_pro, n_extra, prologue, epilogue):
    x_ref, w_ref = refs[0], refs[1]
    pro_refs = refs[2:2 + n_pro]
    extra_refs = refs[2 + n_pro:2 + n_pro + n_extra]
    o_ref = refs[2 + n_pro + n_extra]
    if prologue is not None:
        xs_ref = refs[3 + n_pro + n_extra]

        @pl.when(pl.program_id(1) == 0)
        def _():
            xs_ref[...] = prologue(x_ref, *pro_refs).astype(xs_ref.dtype)

        xv = xs_ref[...]
    else:
        xv = x_ref[...]
    acc = jnp.dot(xv, w_ref[...].astype(BF16), preferred_element_type=F32)
    epilogue(acc, o_ref, *extra_refs)


def _store_epilogue(acc, o_ref):
    o_ref[...] = acc.astype(o_ref.dtype)


def _matmul(x, w, *, m, k, n, bm, bn, out_cols, out_bn, out_dtype, name,
            x_col_block=0, pro=(), pro_specs=(), prologue=None,
            extra=(), extra_specs=(), epilogue=_store_epilogue):
    assert m % bm == 0 and n % bn == 0
    scratch = [pltpu.VMEM((bm, k), BF16)] if prologue is not None else []
    kern = functools.partial(_mm_kernel, n_pro=len(pro), n_extra=len(extra),
                             prologue=prologue, epilogue=epilogue)
    return pl.pallas_call(
        kern,
        grid=(m // bm, n // bn),
        in_specs=[pl.BlockSpec((bm, k), lambda i, j: (i, x_col_block)),
                  pl.BlockSpec((k, bn), lambda i, j: (0, j)),
                  *pro_specs, *extra_specs],
        out_specs=pl.BlockSpec((bm, out_bn), lambda i, j: (i, j)),
        out_shape=jax.ShapeDtypeStruct((m, out_cols), out_dtype),
        scratch_shapes=scratch,
        compiler_params=_params("parallel", "arbitrary"),
        name=name,
    )(x, w, *pro, *extra)


def _rope_tile(v, ct, s1, s2, half):
    return v * ct + pltpu.roll(v, LANES - half, 1) * s1 + pltpu.roll(v, half, 1) * s2


def _kv_prep_kernel(ukv_ref, ukr_ref, g_ref, ct_ref, s1_ref, s2_ref, c32_ref, c16_ref, kpe_ref, *, half):
    c = _rms(ukv_ref[...], g_ref[...])
    c32_ref[...] = c
    c16_ref[...] = c.astype(BF16)
    kpe_ref[...] = _rope_tile(ukr_ref[...], ct_ref[...], s1_ref[...], s2_ref[...], half)


def _softmax_step(q, k, v, carry, mask=None):
    m_i, l_i, acc = carry
    s = lax.dot_general(q, k, (((1,), (1,)), ((), ())), preferred_element_type=F32)
    if mask is not None:
        s = jnp.where(mask, s, MASK_VALUE)
    m_new = jnp.maximum(m_i, jnp.max(s, axis=1, keepdims=True))
    alpha = jnp.exp2(m_i - m_new)
    p = jnp.exp2(s - m_new)
    l_new = alpha * l_i + jnp.sum(p, axis=1, keepdims=True)
    acc_new = alpha * acc + jnp.dot(p.astype(v.dtype), v, preferred_element_type=F32)
    return m_new, l_new, acc_new


def _zero_fill_extra_rows(o_ref, is_extra_step):
    @pl.when(is_extra_step)
    def _():
        o_ref[...] = jnp.zeros(o_ref.shape, o_ref.dtype)


def _attn_tok_kernel(q_ref, k_ref, v_ref, km_ref, vm_ref, o_ref, *, bq, n_meta, n_prompts):
    s_len, vdim = v_ref.shape
    _zero_fill_extra_rows(o_ref, pl.program_id(0) == n_prompts)

    @pl.when(pl.program_id(0) < n_prompts)
    def _():
        meta_mask = lax.broadcasted_iota(jnp.int32, (bq, km_ref.shape[0]), 1) < n_meta
        causal = (lax.broadcasted_iota(jnp.int32, (bq, bq), 1) <= lax.broadcasted_iota(jnp.int32, (bq, bq), 0))
        for qi in range(s_len // bq):
            rows = slice(qi * bq, (qi + 1) * bq)
            q = q_ref[rows, :]
            carry = (jnp.full((bq, 1), MASK_VALUE, F32), jnp.zeros((bq, 1), F32), jnp.zeros((bq, vdim), F32))
            carry = _softmax_step(q, km_ref[...], vm_ref[...], carry, meta_mask)
            for kj in range(qi):
                keys = slice(kj * bq, (kj + 1) * bq)
                carry = _softmax_step(q, k_ref[keys, :], v_ref[keys, :], carry)
            carry = _softmax_step(q, k_ref[rows, :], v_ref[rows, :], carry, causal)
            _, l_i, acc = carry
            o_ref[rows, :] = (acc / l_i).astype(o_ref.dtype)


def _attn_meta_kernel(q_ref, km_ref, vm_ref, o_any, o_ref, *, n_meta):
    del o_any
    n = q_ref.shape[0]
    row = lax.broadcasted_iota(jnp.int32, (n, n), 0)
    col = lax.broadcasted_iota(jnp.int32, (n, n), 1)
    mask = (col <= row) & (col < n_meta)
    carry = (jnp.full((n, 1), MASK_VALUE, F32), jnp.zeros((n, 1), F32),
             jnp.zeros((n, vm_ref.shape[1]), F32))
    _, l_i, acc = _softmax_step(q_ref[...], km_ref[...], vm_ref[...], carry, mask)
    o_ref[...] = (acc / l_i).astype(o_ref.dtype)


def _decode_kernel(pt_ref, ckv_hbm, kpe_hbm, q_ref, new_ref, o_ref,
                   ckv_buf, kpe_buf, sem, kc_ref, kp_ref, m_ref, l_ref, acc_ref,
                   *, pages, streams, page_size, c_dim, r_dim):
    b, j = pl.program_id(0), pl.program_id(1)
    n_b, n_j = pl.num_programs(0), pl.num_programs(1)
    step = b * n_j + j
    slot = step % 2
    per = pages // streams

    def page_copies(bb, jj, sl):
        copies = []
        for p in range(pages):
            page = pt_ref[bb, jj * pages + p]
            copies.append(pltpu.make_async_copy(ckv_hbm.at[page], ckv_buf.at[sl, p], sem.at[sl, 0]))
            copies.append(pltpu.make_async_copy(kpe_hbm.at[page], kpe_buf.at[sl, p], sem.at[sl, 1]))
        return copies

    @pl.when(step == 0)
    def _():
        for cp in page_copies(b, j, slot):
            cp.start()

    @pl.when(step + 1 < n_b * n_j)
    def _():
        wrap = j + 1 == n_j
        for cp in page_copies(jnp.where(wrap, b + 1, b), jnp.where(wrap, 0, j + 1), 1 - slot):
            cp.start()

    for cp in page_copies(b, j, slot):
        cp.wait()

    q = q_ref[0]
    q_lat, q_pe = q[:, :c_dim], q[:, c_dim:c_dim + r_dim]

    @pl.when(j == 0)
    def _():
        new = new_ref[0]
        s_self = jnp.sum(q.astype(F32) * new, axis=1, keepdims=True)
        m_ref[0] = s_self
        l_ref[0] = jnp.ones_like(s_self)
        acc_ref[0] = jnp.broadcast_to(new[:, :c_dim], acc_ref.shape[1:])
        for st in range(1, streams):
            m_ref[st] = jnp.full(m_ref.shape[1:], MASK_VALUE, F32)
            l_ref[st] = jnp.zeros(l_ref.shape[1:], F32)
            acc_ref[st] = jnp.zeros(acc_ref.shape[1:], F32)

    def scores(st):
        for p in range(per):
            keys = slice(p * page_size, (p + 1) * page_size)
            kc_ref[st, keys, :] = ckv_buf[slot, st * per + p].astype(kc_ref.dtype)
            kp_ref[st, :, keys] = kpe_buf[slot, st * per + p].astype(kp_ref.dtype)
        return (lax.dot_general(q_lat, kc_ref[st], (((1,), (1,)), ((), ())), preferred_element_type=F32)
                + jnp.dot(q_pe, kp_ref[st], preferred_element_type=F32))

    def softmax(st, s):
        m_old = m_ref[st]
        m_new = jnp.maximum(m_old, jnp.max(s, axis=1, keepdims=True))
        alpha = jnp.exp2(m_old - m_new)
        p_att = jnp.exp2(s - m_new)
        l_ref[st] = alpha * l_ref[st] + jnp.sum(p_att, axis=1, keepdims=True)
        m_ref[st] = m_new
        return alpha, p_att.astype(kc_ref.dtype)

    def values(st, alpha, p_att):
        acc_ref[st] = alpha * acc_ref[st] + jnp.dot(p_att, kc_ref[st], preferred_element_type=F32)

    pending_s, pending_p = None, None
    for st in range(streams):
        s = scores(st)
        if pending_p is not None:
            values(st - 2, *pending_p)
            pending_p = None
        if pending_s is not None:
            pending_p = softmax(st - 1, pending_s)
        pending_s = s
    if pending_p is not None:
        values(streams - 2, *pending_p)
    values(streams - 1, *softmax(streams - 1, pending_s))

    @pl.when(j == pl.num_programs(1) - 1)
    def _():
        m_all = m_ref[0]
        for st in range(1, streams):
            m_all = jnp.maximum(m_all, m_ref[st])
        l_all = jnp.zeros(l_ref.shape[1:], F32)
        acc_all = jnp.zeros(acc_ref.shape[1:], F32)
        for st in range(streams):
            w = jnp.exp2(m_ref[st] - m_all)
            l_all = l_all + w * l_ref[st]
            acc_all = acc_all + w * acc_ref[st]
        o_ref[0] = (acc_all / l_all).astype(o_ref.dtype)


def _decode_attention(page_table, cache_ckv, cache_kpe_t, q_cat, new_cat, *, pages, streams):
    db, n_pages = page_table.shape
    _, page_size, c_dim = cache_ckv.shape
    r_dim = cache_kpe_t.shape[1]
    n_heads, qk_dim = q_cat.shape[1], q_cat.shape[2]
    assert n_pages % pages == 0 and pages % streams == 0
    per = pages // streams

    grid_spec = pltpu.PrefetchScalarGridSpec(
        num_scalar_prefetch=1,
        grid=(db, n_pages // pages),
        in_specs=[pl.BlockSpec(memory_space=pl.ANY),
                  pl.BlockSpec(memory_space=pl.ANY),
                  pl.BlockSpec((1, n_heads, qk_dim), lambda b, j, pt: (b, 0, 0)),
                  pl.BlockSpec((1, 1, qk_dim), lambda b, j, pt: (b, 0, 0))],
        out_specs=pl.BlockSpec((1, n_heads, c_dim), lambda b, j, pt: (b, 0, 0)),
        scratch_shapes=[pltpu.VMEM((2, pages, page_size, c_dim), cache_ckv.dtype),
                        pltpu.VMEM((2, pages, r_dim, page_size), cache_kpe_t.dtype),
                        pltpu.SemaphoreType.DMA((2, 2)),
                        pltpu.VMEM((streams, per * page_size, c_dim), BF16),
                        pltpu.VMEM((streams, r_dim, per * page_size), BF16),
                        pltpu.VMEM((streams, n_heads, 1), F32),
                        pltpu.VMEM((streams, n_heads, 1), F32),
                        pltpu.VMEM((streams, n_heads, c_dim), F32)],
    )
    kern = functools.partial(_decode_kernel, pages=pages, streams=streams, page_size=page_size,
                             c_dim=c_dim, r_dim=r_dim)
    return pl.pallas_call(
        kern,
        grid_spec=grid_spec,
        out_shape=jax.ShapeDtypeStruct((db, n_heads, c_dim), F32),
        compiler_params=_params("arbitrary", "arbitrary"),
        name="decode_attention",
    )(page_table, cache_ckv, cache_kpe_t, q_cat, new_cat)


def _qlat_kernel(q_ref, w_ref, o_ref):
    o_ref[0] = jnp.dot(q_ref[...], w_ref[0], preferred_element_type=F32)


def _ouv_kernel(x_ref, w_ref, o_any, o_ref):
    del o_any
    o_ref[...] = jnp.dot(x_ref[0], w_ref[...], preferred_element_type=F32).astype(o_ref.dtype)


def _pool_tok_kernel(u_ref, um_ref, w_ref, sc_ref, o_ref, z_ref, t_ref, *, n_meta, n_prompts):
    g = pl.program_id(1)
    is_prompt = pl.program_id(0) < n_prompts
    _zero_fill_extra_rows(o_ref, pl.program_id(0) == n_prompts)
    s_len = u_ref.shape[0]
    zp = POOL_ZERO_ROWS
    end = zp + n_meta + s_len
    z_ref[:zp, :] = jnp.zeros((zp, z_ref.shape[1]), F32)
    t_ref[:zp, :] = jnp.zeros((zp, t_ref.shape[1]), F32)
    z_ref[zp:zp + n_meta, :] = um_ref[...]
    z_ref[zp + n_meta:, :] = u_ref[...]

    def window_sum(win):
        src, dst = z_ref, t_ref
        d = 1
        while d < win:
            dst[zp:end, :] = src[zp:end, :] + src[zp - d:end - d, :]
            src, dst = dst, src
            d *= 2
        return src[zp + n_meta:end, :]

    for gi, win in enumerate(POOL_WINDOWS):
        @pl.when((g == gi) & is_prompt)
        def _(win=win):
            mix = window_sum(win) / float(win) - u_ref[...]
            y = jnp.dot(mix.astype(BF16), w_ref[0], preferred_element_type=F32)
            o_ref[...] = (y * sc_ref[...]).astype(o_ref.dtype)


def _pool_extra_kernel(st_ref, us_ref, um_ref, w_ref, sc_ref, o_any, o_ref, acc_ref, zz_ref,
                       *, n_state, pg, n_meta_pad):
    del o_any
    j = pl.program_id(0)

    @pl.when(j == 0)
    def _():
        acc_ref[...] = jnp.zeros_like(acc_ref)

    for gi, win in enumerate(POOL_WINDOWS):
        cols = slice(gi * pg, (gi + 1) * pg)

        @pl.when(n_state - j <= win - 1)
        def _(cols=cols):
            acc_ref[:, cols] += st_ref[0, :, cols]

    @pl.when(j == n_state - 1)
    def _():
        zpad = zz_ref.shape[0] - n_meta_pad
        zz_ref[:zpad, :] = jnp.zeros((zpad, zz_ref.shape[1]), F32)
        zz_ref[zpad:, :] = um_ref[...]
        t = lax.broadcasted_iota(jnp.int32, (n_meta_pad, 1), 0)
        for gi, win in enumerate(POOL_WINDOWS):
            cols = slice(gi * pg, (gi + 1) * pg)
            us = us_ref[:, cols]
            mix_s = (acc_ref[:, cols] + us) / float(win) - us
            wsum = zz_ref[zpad:, cols]
            for kk in range(1, win):
                wsum = wsum + zz_ref[zpad - kk:zpad - kk + n_meta_pad, cols]
            cnt = jnp.minimum(t + 1, win).astype(F32)
            mix_m = wsum / cnt - um_ref[:, cols]
            w = w_ref[gi]
            sc = sc_ref[:, cols]
            o_ref[:n_meta_pad, cols] = (jnp.dot(mix_m.astype(BF16), w, preferred_element_type=F32)
                                        * sc).astype(o_ref.dtype)
            o_ref[n_meta_pad:, cols] = (jnp.dot(mix_s.astype(BF16), w, preferred_element_type=F32)
                                        * sc).astype(o_ref.dtype)


def _conv3(x0, x1, x2, w_ref, b_ref):
    return b_ref[...] + x0 * w_ref[0:1, :] + x1 * w_ref[1:2, :] + x2 * w_ref[2:3, :]


def _ffn_up_kernel(x_ref, xm_ref, wg_ref, wv_ref, cwg_ref, cwv_ref, cbg_ref, cbv_ref,
                   s0g_ref, s0v_ref, s1g_ref, s1v_ref,
                   act_ref, tailg_ref, tailv_ref, sampg_ref, sampv_ref,
                   xs_ref, upg_a, upv_a, upg_b, upv_b, eg_ref, ev_ref, prevg_ref, prevv_ref,
                   *, bm, n_cols, n_meta, samp_lo, n_samp, tails, starts):
    i, j = pl.program_id(0), pl.program_id(1)
    last_tile = pl.num_programs(0) - 1

    @pl.when(j == 0)
    def _():
        xs_ref[:bm, :] = x_ref[...]
        xs_ref[bm:, :] = xm_ref[...]

    @pl.when((i == 0) & (j == 0))
    def _():
        prevg_ref[...] = jnp.zeros(prevg_ref.shape, F32)
        prevv_ref[...] = jnp.zeros(prevv_ref.shape, F32)

    k_chunk = xs_ref.shape[1] // N_K_CHUNKS
    groups = bm // 16
    piece_rows = [(groups // N_ROW_PIECES + (1 if p < groups % N_ROW_PIECES else 0)) * 16
                  for p in range(N_ROW_PIECES)]
    piece_start = [sum(piece_rows[:p]) for p in range(N_ROW_PIECES)]

    def product_chunk(c, acc):
        ks = slice(c * k_chunk, (c + 1) * k_chunk)
        xk = xs_ref[:, ks]
        pg = jnp.dot(xk, wg_ref[ks, :].astype(BF16), preferred_element_type=F32)
        pv = jnp.dot(xk, wv_ref[ks, :].astype(BF16), preferred_element_type=F32)
        return (pg, pv) if acc is None else (acc[0] + pg, acc[1] + pv)

    def finish_rows(p, up_g, up_v):
        jj = j - 1
        r0, n = piece_start[p], piece_rows[p]
        r1 = r0 + n

        def conv(up_ref, e_ref, prev_ref, cw_ref, cb_ref):
            if r0 == 0:
                e_ref[0:8, :] = prev_ref[jj]
            e_ref[8 + r0:8 + r1, :] = up_ref[r0:r1, :]
            if r1 == bm:
                prev_ref[jj] = up_ref[bm - 8:bm, :]
            return _conv3(e_ref[6 + r0:6 + r1, :], e_ref[7 + r0:7 + r1, :], up_ref[r0:r1, :], cw_ref, cb_ref)

        gate = conv(up_g, eg_ref, prevg_ref, cwg_ref, cbg_ref)
        val = conv(up_v, ev_ref, prevv_ref, cwv_ref, cbv_ref)
        act_ref[r0:r1, :] = (jax.nn.silu(gate) * val).astype(act_ref.dtype)

    def redo_sequence_start(lo, after_meta, up_g, up_v):
        def conv(up_ref, e_ref, cw_ref, cb_ref):
            if after_meta:
                e_ref[0:8, :] = up_ref[bm + n_meta - 8:bm + n_meta, :]
            else:
                e_ref[0:8, :] = jnp.zeros((8, e_ref.shape[1]), F32)
            e_ref[8:16, :] = up_ref[lo:lo + 8, :]
            return _conv3(e_ref[6:14, :], e_ref[7:15, :], up_ref[lo:lo + 8, :], cw_ref, cb_ref)

        gate = conv(up_g, eg_ref, cwg_ref, cbg_ref)
        val = conv(up_v, ev_ref, cwv_ref, cbv_ref)
        act_ref[lo:lo + 8, :] = (jax.nn.silu(gate) * val).astype(act_ref.dtype)

    def run(write, read):
        acc = None
        per = N_K_CHUNKS // N_ROW_PIECES
        for c in range(N_K_CHUNKS):
            if write is not None:
                acc = product_chunk(c, acc)
            if read is not None and c % per == 0:
                finish_rows(c // per, *read)
        if write is not None:
            write[0][...] = acc[0]
            write[1][...] = acc[1]
        if read is not None:
            finish_special_rows(*read)

    def finish_special_rows(up_g, up_v):
        for tile, lo, after_meta in starts:
            @pl.when(i == tile)
            def _(lo=lo, after_meta=after_meta):
                redo_sequence_start(lo, after_meta, up_g, up_v)

        samp = slice(samp_lo, samp_lo + n_samp)

        @pl.when(i == last_tile)
        def _():
            gate_s = _conv3(s0g_ref[...], s1g_ref[...], up_g[samp, :], cwg_ref, cbg_ref)
            val_s = _conv3(s0v_ref[...], s1v_ref[...], up_v[samp, :], cwv_ref, cbv_ref)
            act_ref[samp, :] = (jax.nn.silu(gate_s) * val_s).astype(act_ref.dtype)
            sampg_ref[0] = up_g[samp, :]
            sampv_ref[0] = up_v[samp, :]

        @pl.when(i != last_tile)
        def _():
            sampg_ref[0] = jnp.zeros(sampg_ref.shape[1:], F32)
            sampv_ref[0] = jnp.zeros(sampv_ref.shape[1:], F32)

        has_tail = i == tails[0][0]
        for tile, lo in tails:
            has_tail = has_tail | (i == tile)

            @pl.when(i == tile)
            def _(lo=lo):
                tailg_ref[0] = up_g[lo:lo + 8, :]
                tailv_ref[0] = up_v[lo:lo + 8, :]

        @pl.when(jnp.logical_not(has_tail))
        def _():
            tailg_ref[0] = jnp.zeros(tailg_ref.shape[1:], F32)
            tailv_ref[0] = jnp.zeros(tailv_ref.shape[1:], F32)

    buffers = ((upg_a, upv_a), (upg_b, upv_b))
    for parity in (0, 1):
        write, read = buffers[parity], buffers[1 - parity]
        is_mine = j % 2 == parity
        if parity == 0:
            @pl.when(j == 0)
            def _(write=write):
                run(write, None)

        @pl.when(is_mine & (j > 0) & (j < n_cols))
        def _(write=write, read=read):
            run(write, read)

        if n_cols % 2 == parity:
            @pl.when(j == n_cols)
            def _(read=read):
                run(None, read)


def kernel(x_prompt, x_sample, cache_ckv, cache_kpe, state_pool, state_conv, page_table, meta_tokens,
           norm_mix, w_in, g_qnorm, w_uq, g_kvnorm, w_uk, w_uv, w_attn_branch, w_pool_grp, pool_scale,
           w_pool_branch, w_out, norm_ffn, w_up, conv_w, conv_b, w_down, norm_final):
    nb, s_len, d = x_prompt.shape
    db, dec_t, _ = x_sample.shape
    depth = w_in.shape[0]
    assert depth == 1 and dec_t == 1
    nm = meta_tokens.shape[0]
    _, c_dim, n_heads, d_nope = w_uk.shape
    v_dim = w_uv.shape[3]
    r_dim = cache_kpe.shape[3]
    half = r_dim // 2
    q_lora = g_qnorm.shape[1]
    pw = pool_scale.shape[1]
    n_grp = len(POOL_WINDOWS)
    pg = pw // n_grp
    n_state = state_pool.shape[2]
    conv_taps = conv_w.shape[1]
    dff2 = w_up.shape[2]
    dff = dff2 // 2
    page_size = cache_ckv.shape[2]
    n_pages = page_table.shape[1]
    past = n_pages * page_size
    scale = float(d_nope + r_dim) ** -0.5 * LOG2_E
    hd = 2 * LANES
    assert d_nope == LANES and v_dim == LANES and r_dim <= LANES and conv_taps == 3
    assert nm >= max(POOL_WINDOWS) - 1 and nm % 16 == 0 and nm <= META_PAD
    assert n_state == max(POOL_WINDOWS) - 1

    rt = nb * s_len
    r_meta = rt
    r_samp = rt + META_PAD
    e_rows = META_PAD + db
    m_rows = rt + e_rows
    assert rt % META_PAD == 0 and r_samp % db == 0 and db % 16 == 0 and r_meta % nm == 0

    off_q, off_kv, off_kr, off_gp = pw, pw + q_lora, pw + q_lora + c_dim, pw + q_lora + c_dim + r_dim
    assert off_q % q_lora == 0 and off_kv % c_dim == 0 and off_kr % LANES == 0 and pw % pg == 0

    n_a = off_kr + LANES
    n_a_pad = -(-n_a // 256) * 256
    w_a = jnp.concatenate([w_in[0, :, :off_gp], jnp.zeros((d, n_a_pad - off_gp), F32)], axis=1).astype(BF16)
    w_g = w_in[0, :, off_gp:].astype(BF16)
    wq = w_uq[0].reshape(q_lora, n_heads, d_nope + r_dim)
    wq = jnp.concatenate([wq, jnp.zeros((q_lora, n_heads, hd - d_nope - r_dim), F32)], axis=2)
    wq = wq.reshape(q_lora, n_heads * hd).astype(BF16)
    w_uk2 = w_uk[0].reshape(c_dim, n_heads * d_nope).astype(BF16)
    w_ukt = jnp.transpose(w_uk[0], (1, 2, 0)).astype(BF16)
    w_uv2 = w_uv[0].reshape(c_dim, n_heads * v_dim).astype(BF16)
    w_pg = w_pool_grp[0].astype(BF16)
    w_d = w_down[0].astype(BF16)
    w_ab, w_pb, w_o, w_u = w_attn_branch[0], w_pool_branch[0], w_out[0], w_up[0]

    pos = jnp.concatenate([
        jnp.tile(nm + jnp.arange(s_len), nb),
        jnp.arange(nm), jnp.zeros((META_PAD - nm,), jnp.int32),
        jnp.full((db,), past, jnp.int32)])
    inv = ROPE_BASE ** (-jnp.arange(0, r_dim, 2, dtype=F32) / r_dim)
    ang = pos.astype(F32)[:, None] * inv[None, :]
    cos, sin = jnp.cos(ang), jnp.sin(ang)
    zl = jnp.zeros((m_rows, LANES - 2 * half), F32)
    zh = jnp.zeros((m_rows, half), F32)
    tab_c = jnp.concatenate([cos, cos, zl], axis=1)
    tab_s1 = jnp.concatenate([-sin, zh, zl], axis=1)
    tab_s2 = jnp.concatenate([zh, sin, zl], axis=1)

    bm = _pick_block(m_rows, 1100, 16)
    row_tab_spec = pl.BlockSpec((bm, LANES), lambda i, j: (i, 0))

    x_extra = jnp.concatenate([meta_tokens.astype(F32), jnp.zeros((META_PAD - nm, d), F32),
                               x_sample.reshape(db, d)], axis=0)
    rb = _pick_block(e_rows, 256, 16)
    assert rt % rb == 0
    x_all, h = _norm_in(x_prompt.reshape(rt, d), x_extra, norm_mix[0], rb)

    bn_a = _pick_block(n_a_pad, 768, LANES)
    u1 = _matmul(h, w_a, m=m_rows, k=d, n=n_a_pad, bm=bm, bn=bn_a, out_cols=n_a_pad, out_bn=bn_a,
                 out_dtype=F32, name="in_proj")

    def gate_epilogue(acc, o_ref):
        o_ref[...] = jax.nn.sigmoid(acc).astype(o_ref.dtype)

    bn_d = _pick_block(d, 512, LANES)
    bn_g = _pick_block(2 * d, 1024, LANES)
    gates = _matmul(h, w_g, m=m_rows, k=d, n=2 * d, bm=bm, bn=bn_g, out_cols=2 * d, out_bn=bn_g,
                    out_dtype=BF16, name="gate_proj", epilogue=gate_epilogue)

    def q_prologue(x_ref, g_ref):
        return _rms(x_ref[...], g_ref[...])

    bn_q = _pick_block(n_heads * hd, 1024, hd)

    def q_epilogue(acc, o_ref, ct_ref, s1_ref, s2_ref):
        acc = acc * scale
        for cb in range(bn_q // LANES):
            cols = slice(cb * LANES, (cb + 1) * LANES)
            v = acc[:, cols]
            if cb % 2 == 1:
                v = _rope_tile(v, ct_ref[...], s1_ref[...], s2_ref[...], half)
            o_ref[:, cols] = v.astype(o_ref.dtype)

    q_all = _matmul(u1, wq, m=m_rows, k=q_lora, n=n_heads * hd, bm=bm, bn=bn_q,
                    out_cols=n_heads * hd, out_bn=bn_q, out_dtype=BF16, name="q_proj",
                    x_col_block=off_q // q_lora,
                    pro=(g_qnorm[0].reshape(1, q_lora),), pro_specs=(pl.BlockSpec((1, q_lora), lambda i, j: (0, 0)),),
                    prologue=q_prologue,
                    extra=(tab_c, tab_s1, tab_s2), extra_specs=(row_tab_spec,) * 3, epilogue=q_epilogue)

    tab1 = pl.BlockSpec((bm, LANES), lambda i: (i, 0))
    ckv32, ckv16, kpe32 = pl.pallas_call(
        functools.partial(_kv_prep_kernel, half=half),
        grid=(m_rows // bm,),
        in_specs=[pl.BlockSpec((bm, c_dim), lambda i: (i, off_kv // c_dim)),
                  pl.BlockSpec((bm, LANES), lambda i: (i, off_kr // LANES)),
                  pl.BlockSpec((1, c_dim), lambda i: (0, 0)),
                  tab1, tab1, tab1],
        out_specs=[pl.BlockSpec((bm, c_dim), lambda i: (i, 0)),
                   pl.BlockSpec((bm, c_dim), lambda i: (i, 0)),
                   pl.BlockSpec((bm, LANES), lambda i: (i, 0))],
        out_shape=[jax.ShapeDtypeStruct((m_rows, c_dim), F32),
                   jax.ShapeDtypeStruct((m_rows, c_dim), BF16),
                   jax.ShapeDtypeStruct((m_rows, LANES), F32)],
        compiler_params=_params("arbitrary"),
        name="kv_prep",
    )(u1, u1, g_kvnorm[0].reshape(1, c_dim), tab_c, tab_s1, tab_s2)

    hps = _pick_block(n_heads, 16, 1)

    def k_epilogue(acc, o_ref, kpe_ref):
        kr = kpe_ref[...].astype(o_ref.dtype)
        for hh in range(hps):
            o_ref[:, hh * hd:hh * hd + d_nope] = acc[:, hh * d_nope:(hh + 1) * d_nope].astype(o_ref.dtype)
            o_ref[:, hh * hd + d_nope:(hh + 1) * hd] = kr

    k_all = _matmul(ckv16, w_uk2, m=m_rows, k=c_dim, n=n_heads * d_nope, bm=bm, bn=hps * d_nope,
                    out_cols=n_heads * hd, out_bn=hps * hd, out_dtype=BF16, name="k_up",
                    extra=(kpe32,), extra_specs=(row_tab_spec,), epilogue=k_epilogue)
    bn_v = _pick_block(n_heads * v_dim, 2048, LANES)
    v_all = _matmul(ckv16, w_uv2, m=m_rows, k=c_dim, n=n_heads * v_dim, bm=bm, bn=bn_v,
                    out_cols=n_heads * v_dim, out_bn=bn_v, out_dtype=BF16, name="v_up")

    bq = _pick_block(s_len, 512, 128)
    mb = r_meta // META_PAD
    assert e_rows <= s_len

    def prompt_block(b, col, col_off=0):
        return jnp.minimum(b, nb - 1), jnp.where(b < nb, col, 0) + col_off
    o_all = pl.pallas_call(
        functools.partial(_attn_tok_kernel, bq=bq, n_meta=nm, n_prompts=nb),
        grid=(nb + 1, n_heads),
        in_specs=[pl.BlockSpec((s_len, hd), prompt_block),
                  pl.BlockSpec((s_len, hd), prompt_block),
                  pl.BlockSpec((s_len, v_dim), prompt_block),
                  pl.BlockSpec((META_PAD, hd), lambda b, hh: (mb, hh)),
                  pl.BlockSpec((META_PAD, v_dim), lambda b, hh: (mb, hh))],
        out_specs=pl.BlockSpec((s_len, v_dim), lambda b, hh: (b, hh)),
        out_shape=jax.ShapeDtypeStruct((m_rows, n_heads * v_dim), BF16),
        compiler_params=_params("parallel", "arbitrary"),
        name="attn_prompt",
    )(q_all, k_all, v_all, k_all, v_all)

    o_all = pl.pallas_call(
        functools.partial(_attn_meta_kernel, n_meta=nm),
        grid=(n_heads,),
        in_specs=[pl.BlockSpec((META_PAD, hd), lambda hh: (mb, hh)),
                  pl.BlockSpec((META_PAD, hd), lambda hh: (mb, hh)),
                  pl.BlockSpec((META_PAD, v_dim), lambda hh: (mb, hh)),
                  pl.BlockSpec(memory_space=pl.ANY)],
        out_specs=pl.BlockSpec((META_PAD, v_dim), lambda hh: (mb, hh)),
        out_shape=jax.ShapeDtypeStruct((m_rows, n_heads * v_dim), BF16),
        input_output_aliases={3: 0},
        compiler_params=_params("arbitrary"),
        name="attn_meta",
    )(q_all, k_all, v_all, o_all)

    sb = r_samp // db
    q_lat = pl.pallas_call(
        _qlat_kernel,
        grid=(n_heads,),
        in_specs=[pl.BlockSpec((db, d_nope), lambda hh: (sb, hh * (hd // d_nope))),
                  pl.BlockSpec((1, d_nope, c_dim), lambda hh: (hh, 0, 0))],
        out_specs=pl.BlockSpec((1, db, c_dim), lambda hh: (hh, 0, 0)),
        out_shape=jax.ShapeDtypeStruct((n_heads, db, c_dim), F32),
        compiler_params=_params("arbitrary"),
        name="q_absorb",
    )(q_all, w_ukt)
    qk_dim = -(-(c_dim + r_dim) // LANES) * LANES
    q_samp = q_all[r_samp:r_samp + db].reshape(db, n_heads, hd)
    q_cat = jnp.concatenate([jnp.transpose(q_lat, (1, 0, 2)).astype(BF16),
                             q_samp[:, :, d_nope:d_nope + r_dim],
                             jnp.zeros((db, n_heads, qk_dim - c_dim - r_dim), BF16)], axis=2)
    new_cat = jnp.concatenate([ckv32[r_samp:r_samp + db], kpe32[r_samp:r_samp + db, :r_dim],
                               jnp.zeros((db, qk_dim - c_dim - r_dim), F32)], axis=1).reshape(db, 1, qk_dim)
    pages = _pick_block(n_pages, 32, 1)
    streams = _pick_block(pages, 4, 1)
    cache_kpe_t = jnp.swapaxes(cache_kpe[0], 1, 2)
    o_lat = _decode_attention(page_table, cache_ckv[0], cache_kpe_t, q_cat, new_cat, pages=pages, streams=streams)
    o_lat_t = jnp.transpose(o_lat, (1, 0, 2)).astype(BF16)
    o_all = pl.pallas_call(
        _ouv_kernel,
        grid=(n_heads,),
        in_specs=[pl.BlockSpec((1, db, c_dim), lambda hh: (hh, 0, 0)),
                  pl.BlockSpec((c_dim, v_dim), lambda hh: (0, hh)),
                  pl.BlockSpec(memory_space=pl.ANY)],
        out_specs=pl.BlockSpec((db, v_dim), lambda hh: (sb, hh)),
        out_shape=jax.ShapeDtypeStruct((m_rows, n_heads * v_dim), BF16),
        input_output_aliases={2: 0},
        compiler_params=_params("arbitrary"),
        name="o_up_decode",
    )(o_lat_t, w_uv2, o_all)

    scale_row = pool_scale[0].reshape(1, pw)
    pm = pl.pallas_call(
        functools.partial(_pool_tok_kernel, n_meta=nm, n_prompts=nb),
        grid=(nb + 1, n_grp),
        in_specs=[pl.BlockSpec((s_len, pg), prompt_block),
                  pl.BlockSpec((nm, pg), lambda b, g: (r_meta // nm, g)),
                  pl.BlockSpec((1, pg, pg), lambda b, g: (g, 0, 0)),
                  pl.BlockSpec((1, pg), lambda b, g: (0, g))],
        out_specs=pl.BlockSpec((s_len, pg), lambda b, g: (b, g)),
        out_shape=jax.ShapeDtypeStruct((m_rows, pw), BF16),
        scratch_shapes=[pltpu.VMEM((POOL_ZERO_ROWS + nm + s_len, pg), F32),
                        pltpu.VMEM((POOL_ZERO_ROWS + nm + s_len, pg), F32)],
        compiler_params=_params("parallel", "arbitrary"),
        name="pool_prompt",
    )(u1, u1, w_pg, scale_row)
    st_rows = jnp.transpose(state_pool[0], (1, 0, 2))
    zrows = POOL_ZERO_ROWS
    pm = pl.pallas_call(
        functools.partial(_pool_extra_kernel, n_state=n_state, pg=pg, n_meta_pad=META_PAD),
        grid=(n_state,),
        in_specs=[pl.BlockSpec((1, db, pw), lambda j: (j, 0, 0)),
                  pl.BlockSpec((db, pw), lambda j: (sb, 0)),
                  pl.BlockSpec((META_PAD, pw), lambda j: (mb, 0)),
                  pl.BlockSpec((n_grp, pg, pg), lambda j: (0, 0, 0)),
                  pl.BlockSpec((1, pw), lambda j: (0, 0)),
                  pl.BlockSpec(memory_space=pl.ANY)],
        out_specs=pl.BlockSpec((e_rows, pw), lambda j: (rt // e_rows, 0)),
        out_shape=jax.ShapeDtypeStruct((m_rows, pw), BF16),
        scratch_shapes=[pltpu.VMEM((db, pw), F32), pltpu.VMEM((zrows + META_PAD, pw), F32)],
        input_output_aliases={5: 0},
        compiler_params=_params("arbitrary"),
        name="pool_extra",
    )(st_rows, u1, u1, w_pg, scale_row, pm)

    def pool_branch_epilogue(acc, o_ref, g_ref):
        o_ref[...] = (g_ref[...].astype(F32) * acc).astype(o_ref.dtype)

    tile_spec = pl.BlockSpec((bm, bn_d), lambda i, j: (i, j))
    t1 = _matmul(pm, w_pb, m=m_rows, k=pw, n=d, bm=bm, bn=bn_d, out_cols=d, out_bn=bn_d, out_dtype=BF16,
                 name="pool_branch", extra=(gates,), extra_specs=(tile_spec,), epilogue=pool_branch_epilogue)

    def attn_branch_epilogue(acc, o_ref, g_ref, t_ref):
        o_ref[...] = (t_ref[...].astype(F32) + g_ref[...].astype(F32) * acc).astype(o_ref.dtype)

    mixed = _matmul(o_all, w_ab, m=m_rows, k=n_heads * v_dim, n=d, bm=bm, bn=bn_d, out_cols=d, out_bn=bn_d,
                    out_dtype=BF16, name="attn_branch", extra=(gates, t1),
                    extra_specs=(pl.BlockSpec((bm, bn_d), lambda i, j: (i, j + d // bn_d)), tile_spec),
                    epilogue=attn_branch_epilogue)

    def residual_epilogue(acc, o_ref, r_ref):
        o_ref[...] = r_ref[...] + acc

    x1 = _matmul(mixed, w_o, m=m_rows, k=d, n=d, bm=bm, bn=bn_d, out_cols=d, out_bn=bn_d, out_dtype=F32,
                 name="out_proj", extra=(x_all,), extra_specs=(tile_spec,), epilogue=residual_epilogue)

    rb2 = _pick_block(m_rows, 512, 16)
    h2 = _norm_rows(x1, norm_ffn[0], row0=0, nrows=m_rows, rb=rb2, out_dtype=BF16, name="norm_ffn")
    bc = _pick_block(dff, 256, LANES)
    ncb = dff // bc
    n_mt = m_rows // bm
    last_mt = n_mt - 1
    samp_lo = r_samp - last_mt * bm
    assert samp_lo >= 0 and r_samp + db == m_rows
    tails = tuple(divmod((b + 1) * s_len - 8, bm) for b in range(nb))
    assert len({t for t, _ in tails}) == nb and all(lo + 8 <= bm for _, lo in tails)
    starts = tuple((*divmod(b * s_len, bm), True) for b in range(nb)) + ((*divmod(rt, bm), False),)
    assert nm >= 8 and all(lo % 8 == 0 for _, lo, _ in starts)
    cw, cb = conv_w[0], conv_b[0].reshape(1, dff2)
    sc2d = state_conv[0].reshape(db, (conv_taps - 1) * dff2)

    def done_block(j):
        return jnp.maximum(j - 1, 0)

    def state_spec(off):
        return pl.BlockSpec((db, bc), lambda i, j: (0, jnp.where(i == last_mt, done_block(j), 0) + off))

    samp_spec = pl.BlockSpec((1, db, bc), lambda i, j: (jnp.where(i == last_mt, last_mt + done_block(j), i), 0, 0))
    tail_spec = pl.BlockSpec((1, 8, bc), lambda i, j: (i, 0, done_block(j)))
    act, tail_g, tail_v, samp_g, samp_v = pl.pallas_call(
        functools.partial(_ffn_up_kernel, bm=bm, n_cols=ncb, n_meta=nm,
                          samp_lo=samp_lo, n_samp=db, tails=tails, starts=starts),
        grid=(n_mt, ncb + 1),
        in_specs=[pl.BlockSpec((bm, d), lambda i, j: (i, 0), pipeline_mode=pl.Buffered(1)),
                  pl.BlockSpec((nm, d), lambda i, j: (r_meta // nm, 0)),
                  pl.BlockSpec((d, bc), lambda i, j: (0, jnp.minimum(j, ncb - 1))),
                  pl.BlockSpec((d, bc), lambda i, j: (0, jnp.minimum(j, ncb - 1) + ncb)),
                  pl.BlockSpec((conv_taps, bc), lambda i, j: (0, done_block(j))),
                  pl.BlockSpec((conv_taps, bc), lambda i, j: (0, done_block(j) + ncb)),
                  pl.BlockSpec((1, bc), lambda i, j: (0, done_block(j))),
                  pl.BlockSpec((1, bc), lambda i, j: (0, done_block(j) + ncb)),
                  state_spec(0), state_spec(ncb), state_spec(2 * ncb), state_spec(3 * ncb)],
        out_specs=[pl.BlockSpec((bm, bc), lambda i, j: (i, done_block(j))),
                   tail_spec, tail_spec, samp_spec, samp_spec],
        out_shape=[jax.ShapeDtypeStruct((m_rows, dff), BF16),
                   jax.ShapeDtypeStruct((n_mt, 8, dff), F32),
                   jax.ShapeDtypeStruct((n_mt, 8, dff), F32),
                   jax.ShapeDtypeStruct((last_mt + ncb, db, bc), F32),
                   jax.ShapeDtypeStruct((last_mt + ncb, db, bc), F32)],
        scratch_shapes=[pltpu.VMEM((bm + nm, d), BF16),
                        pltpu.VMEM((bm + nm, bc), F32), pltpu.VMEM((bm + nm, bc), F32),
                        pltpu.VMEM((bm + nm, bc), F32), pltpu.VMEM((bm + nm, bc), F32),
                        pltpu.VMEM((8 + bm, bc), F32), pltpu.VMEM((8 + bm, bc), F32),
                        pltpu.VMEM((ncb, 8, bc), F32), pltpu.VMEM((ncb, 8, bc), F32)],
        compiler_params=_params("arbitrary", "arbitrary"),
        name="ffn_up",
    )(h2, h2, w_u, w_u, cw, cw, cb, cb, sc2d, sc2d, sc2d, sc2d)
    n_tail = conv_taps - 1
    tail_tiles = jnp.array([t for t, _ in tails])
    up_tail = jnp.concatenate([tail_g[tail_tiles, 8 - n_tail:], tail_v[tail_tiles, 8 - n_tail:]], axis=2)

    def sample_cols(a):
        return jnp.transpose(a[last_mt:], (1, 0, 2)).reshape(db, dff)

    up_samp = jnp.concatenate([sample_cols(samp_g), sample_cols(samp_v)], axis=1)

    bm_d = _pick_block(m_rows, 528, 16)
    bn_dn = _pick_block(d, 512, LANES)
    x2 = _matmul(act, w_d, m=m_rows, k=dff, n=d, bm=bm_d, bn=bn_dn, out_cols=d, out_bn=bn_dn, out_dtype=F32,
                 name="ffn_down", extra=(x1,), extra_specs=(pl.BlockSpec((bm_d, bn_dn), lambda i, j: (i, j)),),
                 epilogue=residual_epilogue)

    rb3 = _pick_block(rt, 512, 16)
    y_prompt = _norm_rows(x2, norm_final, row0=0, nrows=rt, rb=rb3, out_dtype=F32, name="norm_final_prompt")
    y_sample = _norm_rows(x2, norm_final, row0=r_samp, nrows=db, rb=db, out_dtype=F32, name="norm_final_sample")

    def with_meta(a, width):
        tok = a[:rt, :width].reshape(nb, s_len, width)
        meta = jnp.broadcast_to(a[r_meta:r_meta + nm, :width][None], (nb, nm, width))
        return jnp.concatenate([meta, tok], axis=1)[None]

    def prompt_tail(a, n_tail, width):
        return jnp.stack([a[(b + 1) * s_len - n_tail:(b + 1) * s_len, :width] for b in range(nb)])[None]

    return (
        y_prompt.reshape(nb, s_len, d),
        y_sample.reshape(db, 1, d),
        with_meta(ckv32, c_dim),
        with_meta(kpe32, r_dim),
        prompt_tail(u1, n_state, pw),
        up_tail[None],
        ckv32[r_samp:r_samp + db].reshape(1, db, 1, c_dim),
        kpe32[r_samp:r_samp + db, :r_dim].reshape(1, db, 1, r_dim),
        jnp.concatenate([state_pool[0, :, 1:], u1[r_samp:r_samp + db, :pw][:, None]], axis=1)[None],
        jnp.concatenate([state_conv[0, :, 1:], up_samp[:, None]], axis=1)[None],
    )
```

```python
import functools

import jax
import jax.numpy as jnp
from jax import lax
from jax.experimental import pallas as pl
from jax.experimental.pallas import tpu as pltpu

EPS = 1e-6
ROPE_BASE = 10000.0
POOL_WINDOWS = (2, 4, 8, 16)
LANES = 128
META_PAD = 128
POOL_ZERO_ROWS = 16
MASK_VALUE = -1e30
LOG2_E = 1.4426950408889634
N_K_CHUNKS = 16
N_ROW_PIECES = 8
VMEM_LIMIT_BYTES = 60 * 1024 * 1024

F32 = jnp.float32
BF16 = jnp.bfloat16


def _params(*sem):
    return pltpu.CompilerParams(dimension_semantics=sem, vmem_limit_bytes=VMEM_LIMIT_BYTES)


def _pick_block(n, cap, mult):
    best = None
    for d in range(mult, min(n, cap) + 1, mult):
        if n % d == 0:
            best = d
    assert best is not None, (n, cap, mult)
    return best


def _rms(x, g):
    return x * lax.rsqrt(jnp.mean(x * x, axis=-1, keepdims=True) + EPS) * g


def _norm_in_kernel(xt_ref, xe_ref, g_ref, x_ref, h_ref, *, n_tok_blocks):
    i = pl.program_id(0)

    def emit(src):
        x = src[...]
        x_ref[...] = x
        h_ref[...] = _rms(x, g_ref[...]).astype(h_ref.dtype)

    @pl.when(i < n_tok_blocks)
    def _():
        emit(xt_ref)

    @pl.when(i >= n_tok_blocks)
    def _():
        emit(xe_ref)


def _norm_in(x_tok, x_extra, g, rb):
    rt, d = x_tok.shape
    e = x_extra.shape[0]
    m = rt + e
    ntb, neb = rt // rb, e // rb
    return pl.pallas_call(
        functools.partial(_norm_in_kernel, n_tok_blocks=ntb),
        grid=(ntb + neb,),
        in_specs=[
            pl.BlockSpec((rb, d), lambda i: (jnp.minimum(i, ntb - 1), 0)),
            pl.BlockSpec((rb, d), lambda i: (jnp.maximum(i - ntb, 0), 0)),
            pl.BlockSpec((1, d), lambda i: (0, 0)),
        ],
        out_specs=[pl.BlockSpec((rb, d), lambda i: (i, 0)),
                   pl.BlockSpec((rb, d), lambda i: (i, 0))],
        out_shape=[jax.ShapeDtypeStruct((m, d), F32), jax.ShapeDtypeStruct((m, d), BF16)],
        compiler_params=_params("arbitrary"),
        name="norm_in",
    )(x_tok, x_extra, g.reshape(1, d))


def _norm_rows_kernel(x_ref, g_ref, o_ref):
    o_ref[...] = _rms(x_ref[...], g_ref[...]).astype(o_ref.dtype)


def _norm_rows(x, g, *, row0, nrows, rb, out_dtype, name):
    d = x.shape[1]
    assert row0 % rb == 0 and nrows % rb == 0
    b0 = row0 // rb
    return pl.pallas_call(
        _norm_rows_kernel,
        grid=(nrows // rb,),
        in_specs=[pl.BlockSpec((rb, d), lambda i: (b0 + i, 0)),
                  pl.BlockSpec((1, d), lambda i: (0, 0))],
        out_specs=pl.BlockSpec((rb, d), lambda i: (i, 0)),
        out_shape=jax.ShapeDtypeStruct((nrows, d), out_dtype),
        compiler_params=_params("arbitrary"),
        name=name,
    )(x, g.reshape(1, d))


def _mm_kernel(*refs, n_pro, n_extra, prologue, epilogue, w_is_nk=False):
    x_ref, w_ref = refs[0], refs[1]
    pro_refs = refs[2:2 + n_pro]
    extra_refs = refs[2 + n_pro:2 + n_pro + n_extra]
    o_ref = refs[2 + n_pro + n_extra]
    if prologue is not None:
        xs_ref = refs[3 + n_pro + n_extra]

        @pl.when(pl.program_id(1) == 0)
        def _():
            xs_ref[...] = prologue(x_ref, *pro_refs).astype(xs_ref.dtype)

        xv = xs_ref[...]
    else:
        xv = x_ref[...]
    contract = (((1,), (1,)), ((), ())) if w_is_nk else (((1,), (0,)), ((), ()))
    acc = lax.dot_general(xv, w_ref[...].astype(BF16), contract, preferred_element_type=F32)
    epilogue(acc, o_ref, *extra_refs)


def _store_epilogue(acc, o_ref):
    o_ref[...] = acc.astype(o_ref.dtype)


def _matmul(x, w, *, m, k, n, bm, bn, out_cols, out_bn, out_dtype, name,
            x_col_block=0, pro=(), pro_specs=(), prologue=None,
            extra=(), extra_specs=(), epilogue=_store_epilogue, w_nk_row0=None):
    assert m % bm == 0 and n % bn == 0
    scratch = [pltpu.VMEM((bm, k), BF16)] if prologue is not None else []
    kern = functools.partial(_mm_kernel, n_pro=len(pro), n_extra=len(extra),
                             prologue=prologue, epilogue=epilogue, w_is_nk=w_nk_row0 is not None)
    if w_nk_row0 is None:
        w_spec = pl.BlockSpec((k, bn), lambda i, j: (0, j))
    else:
        assert w_nk_row0 % 8 == 0
        w_spec = pl.BlockSpec((pl.Element(bn), pl.Element(k)),
                              lambda i, j: (pl.multiple_of(w_nk_row0 + j * bn, 8), 0))
    return pl.pallas_call(
        kern,
        grid=(m // bm, n // bn),
        in_specs=[pl.BlockSpec((bm, k), lambda i, j: (i, x_col_block)),
                  w_spec,
                  *pro_specs, *extra_specs],
        out_specs=pl.BlockSpec((bm, out_bn), lambda i, j: (i, j)),
        out_shape=jax.ShapeDtypeStruct((m, out_cols), out_dtype),
        scratch_shapes=scratch,
        compiler_params=_params("parallel", "arbitrary"),
        name=name,
    )(x, w, *pro, *extra)


def _rope_tile(v, ct, s1, s2, half):
    return v * ct + pltpu.roll(v, LANES - half, 1) * s1 + pltpu.roll(v, half, 1) * s2


def _kv_prep_kernel(ukv_ref, ukr_ref, g_ref, ct_ref, s1_ref, s2_ref, c32_ref, c16_ref, kpe_ref, *, half):
    c = _rms(ukv_ref[...], g_ref[...])
    c32_ref[...] = c
    c16_ref[...] = c.astype(BF16)
    kpe_ref[...] = _rope_tile(ukr_ref[...], ct_ref[...], s1_ref[...], s2_ref[...], half)


def _softmax_step(q, k, v, carry, mask=None):
    m_i, l_i, acc = carry
    s = lax.dot_general(q, k, (((1,), (1,)), ((), ())), preferred_element_type=F32)
    if mask is not None:
        s = jnp.where(mask, s, MASK_VALUE)
    m_new = jnp.maximum(m_i, jnp.max(s, axis=1, keepdims=True))
    alpha = jnp.exp2(m_i - m_new)
    p = jnp.exp2(s - m_new)
    l_new = alpha * l_i + jnp.sum(p, axis=1, keepdims=True)
    acc_new = alpha * acc + jnp.dot(p.astype(v.dtype), v, preferred_element_type=F32)
    return m_new, l_new, acc_new


def _zero_fill_extra_rows(o_ref, is_extra_step):
    @pl.when(is_extra_step)
    def _():
        o_ref[...] = jnp.zeros(o_ref.shape, o_ref.dtype)


def _attn_tok_kernel(q_ref, k_ref, v_ref, km_ref, vm_ref, o_ref, *, bq, n_meta, n_prompts):
    s_len, vdim = v_ref.shape
    _zero_fill_extra_rows(o_ref, pl.program_id(0) == n_prompts)

    @pl.when(pl.program_id(0) < n_prompts)
    def _():
        meta_mask = lax.broadcasted_iota(jnp.int32, (bq, km_ref.shape[0]), 1) < n_meta
        causal = (lax.broadcasted_iota(jnp.int32, (bq, bq), 1) <= lax.broadcasted_iota(jnp.int32, (bq, bq), 0))
        for qi in range(s_len // bq):
            rows = slice(qi * bq, (qi + 1) * bq)
            q = q_ref[rows, :]
            carry = (jnp.full((bq, 1), MASK_VALUE, F32), jnp.zeros((bq, 1), F32), jnp.zeros((bq, vdim), F32))
            carry = _softmax_step(q, km_ref[...], vm_ref[...], carry, meta_mask)
            for kj in range(qi):
                keys = slice(kj * bq, (kj + 1) * bq)
                carry = _softmax_step(q, k_ref[keys, :], v_ref[keys, :], carry)
            carry = _softmax_step(q, k_ref[rows, :], v_ref[rows, :], carry, causal)
            _, l_i, acc = carry
            o_ref[rows, :] = (acc / l_i).astype(o_ref.dtype)


def _attn_meta_kernel(q_ref, km_ref, vm_ref, o_any, o_ref, *, n_meta):
    del o_any
    n = q_ref.shape[0]
    row = lax.broadcasted_iota(jnp.int32, (n, n), 0)
    col = lax.broadcasted_iota(jnp.int32, (n, n), 1)
    mask = (col <= row) & (col < n_meta)
    carry = (jnp.full((n, 1), MASK_VALUE, F32), jnp.zeros((n, 1), F32),
             jnp.zeros((n, vm_ref.shape[1]), F32))
    _, l_i, acc = _softmax_step(q_ref[...], km_ref[...], vm_ref[...], carry, mask)
    o_ref[...] = (acc / l_i).astype(o_ref.dtype)


def _decode_kernel(pt_ref, ckv_hbm, kpe_hbm, q_ref, new_ref, o_ref,
                   ckv_buf, kpe_buf, sem, kc_ref, kp_ref, m_ref, l_ref, acc_ref,
                   *, pages, streams, page_size, c_dim, r_dim):
    b, j = pl.program_id(0), pl.program_id(1)
    n_b, n_j = pl.num_programs(0), pl.num_programs(1)
    step = b * n_j + j
    slot = step % 2
    per = pages // streams

    def page_copies(bb, jj, sl):
        copies = []
        for p in range(pages):
            page = pt_ref[bb, jj * pages + p]
            copies.append(pltpu.make_async_copy(ckv_hbm.at[page], ckv_buf.at[sl, p], sem.at[sl, 0]))
            copies.append(pltpu.make_async_copy(kpe_hbm.at[page], kpe_buf.at[sl, p], sem.at[sl, 1]))
        return copies

    @pl.when(step == 0)
    def _():
        for cp in page_copies(b, j, slot):
            cp.start()

    @pl.when(step + 1 < n_b * n_j)
    def _():
        wrap = j + 1 == n_j
        for cp in page_copies(jnp.where(wrap, b + 1, b), jnp.where(wrap, 0, j + 1), 1 - slot):
            cp.start()

    for cp in page_copies(b, j, slot):
        cp.wait()

    q = q_ref[0]
    q_lat, q_pe = q[:, :c_dim], q[:, c_dim:c_dim + r_dim]

    @pl.when(j == 0)
    def _():
        new = new_ref[0]
        s_self = jnp.sum(q.astype(F32) * new, axis=1, keepdims=True)
        m_ref[0] = s_self
        l_ref[0] = jnp.ones_like(s_self)
        acc_ref[0] = jnp.broadcast_to(new[:, :c_dim], acc_ref.shape[1:])
        for st in range(1, streams):
            m_ref[st] = jnp.full(m_ref.shape[1:], MASK_VALUE, F32)
            l_ref[st] = jnp.zeros(l_ref.shape[1:], F32)
            acc_ref[st] = jnp.zeros(acc_ref.shape[1:], F32)

    def scores(st):
        for p in range(per):
            keys = slice(p * page_size, (p + 1) * page_size)
            kc_ref[st, keys, :] = ckv_buf[slot, st * per + p].astype(kc_ref.dtype)
            kp_ref[st, :, keys] = kpe_buf[slot, st * per + p].astype(kp_ref.dtype)
        return (lax.dot_general(q_lat, kc_ref[st], (((1,), (1,)), ((), ())), preferred_element_type=F32)
                + jnp.dot(q_pe, kp_ref[st], preferred_element_type=F32))

    def softmax(st, s):
        m_old = m_ref[st]
        m_new = jnp.maximum(m_old, jnp.max(s, axis=1, keepdims=True))
        alpha = jnp.exp2(m_old - m_new)
        p_att = jnp.exp2(s - m_new)
        l_ref[st] = alpha * l_ref[st] + jnp.sum(p_att, axis=1, keepdims=True)
        m_ref[st] = m_new
        return alpha, p_att.astype(kc_ref.dtype)

    def values(st, alpha, p_att):
        acc_ref[st] = alpha * acc_ref[st] + jnp.dot(p_att, kc_ref[st], preferred_element_type=F32)

    pending_s, pending_p = None, None
    for st in range(streams):
        s = scores(st)
        if pending_p is not None:
            values(st - 2, *pending_p)
            pending_p = None
        if pending_s is not None:
            pending_p = softmax(st - 1, pending_s)
        pending_s = s
    if pending_p is not None:
        values(streams - 2, *pending_p)
    values(streams - 1, *softmax(streams - 1, pending_s))

    @pl.when(j == pl.num_programs(1) - 1)
    def _():
        m_all = m_ref[0]
        for st in range(1, streams):
            m_all = jnp.maximum(m_all, m_ref[st])
        l_all = jnp.zeros(l_ref.shape[1:], F32)
        acc_all = jnp.zeros(acc_ref.shape[1:], F32)
        for st in range(streams):
            w = jnp.exp2(m_ref[st] - m_all)
            l_all = l_all + w * l_ref[st]
            acc_all = acc_all + w * acc_ref[st]
        o_ref[0] = (acc_all / l_all).astype(o_ref.dtype)


def _decode_attention(page_table, cache_ckv, cache_kpe_t, q_cat, new_cat, *, pages, streams):
    db, n_pages = page_table.shape
    _, page_size, c_dim = cache_ckv.shape
    r_dim = cache_kpe_t.shape[1]
    n_heads, qk_dim = q_cat.shape[1], q_cat.shape[2]
    assert n_pages % pages == 0 and pages % streams == 0
    per = pages // streams

    grid_spec = pltpu.PrefetchScalarGridSpec(
        num_scalar_prefetch=1,
        grid=(db, n_pages // pages),
        in_specs=[pl.BlockSpec(memory_space=pl.ANY),
                  pl.BlockSpec(memory_space=pl.ANY),
                  pl.BlockSpec((1, n_heads, qk_dim), lambda b, j, pt: (b, 0, 0)),
                  pl.BlockSpec((1, 1, qk_dim), lambda b, j, pt: (b, 0, 0))],
        out_specs=pl.BlockSpec((1, n_heads, c_dim), lambda b, j, pt: (b, 0, 0)),
        scratch_shapes=[pltpu.VMEM((2, pages, page_size, c_dim), cache_ckv.dtype),
                        pltpu.VMEM((2, pages, r_dim, page_size), cache_kpe_t.dtype),
                        pltpu.SemaphoreType.DMA((2, 2)),
                        pltpu.VMEM((streams, per * page_size, c_dim), BF16),
                        pltpu.VMEM((streams, r_dim, per * page_size), BF16),
                        pltpu.VMEM((streams, n_heads, 1), F32),
                        pltpu.VMEM((streams, n_heads, 1), F32),
                        pltpu.VMEM((streams, n_heads, c_dim), F32)],
    )
    kern = functools.partial(_decode_kernel, pages=pages, streams=streams, page_size=page_size,
                             c_dim=c_dim, r_dim=r_dim)
    return pl.pallas_call(
        kern,
        grid_spec=grid_spec,
        out_shape=jax.ShapeDtypeStruct((db, n_heads, c_dim), F32),
        compiler_params=_params("arbitrary", "arbitrary"),
        name="decode_attention",
    )(page_table, cache_ckv, cache_kpe_t, q_cat, new_cat)


def _qlat_kernel(q_ref, w_ref, o_ref):
    o_ref[0] = jnp.dot(q_ref[...], w_ref[0], preferred_element_type=F32)


def _ouv_kernel(x_ref, w_ref, o_any, o_ref):
    del o_any
    o_ref[...] = jnp.dot(x_ref[0], w_ref[...], preferred_element_type=F32).astype(o_ref.dtype)


def _pool_tok_kernel(u_ref, um_ref, w_ref, sc_ref, o_ref, z_ref, t_ref, *, n_meta, n_prompts):
    g = pl.program_id(1)
    is_prompt = pl.program_id(0) < n_prompts
    _zero_fill_extra_rows(o_ref, pl.program_id(0) == n_prompts)
    s_len = u_ref.shape[0]
    zp = POOL_ZERO_ROWS
    end = zp + n_meta + s_len
    z_ref[:zp, :] = jnp.zeros((zp, z_ref.shape[1]), F32)
    t_ref[:zp, :] = jnp.zeros((zp, t_ref.shape[1]), F32)
    z_ref[zp:zp + n_meta, :] = um_ref[...]
    z_ref[zp + n_meta:, :] = u_ref[...]

    def window_sum(win):
        src, dst = z_ref, t_ref
        d = 1
        while d < win:
            dst[zp:end, :] = src[zp:end, :] + src[zp - d:end - d, :]
            src, dst = dst, src
            d *= 2
        return src[zp + n_meta:end, :]

    for gi, win in enumerate(POOL_WINDOWS):
        @pl.when((g == gi) & is_prompt)
        def _(win=win):
            mix = window_sum(win) / float(win) - u_ref[...]
            y = jnp.dot(mix.astype(BF16), w_ref[0], preferred_element_type=F32)
            o_ref[...] = (y * sc_ref[...]).astype(o_ref.dtype)


def _pool_extra_kernel(st_ref, us_ref, um_ref, w_ref, sc_ref, o_any, o_ref, acc_ref, zz_ref,
                       *, n_state, pg, n_meta_pad):
    del o_any
    j = pl.program_id(0)

    @pl.when(j == 0)
    def _():
        acc_ref[...] = jnp.zeros_like(acc_ref)

    for gi, win in enumerate(POOL_WINDOWS):
        cols = slice(gi * pg, (gi + 1) * pg)

        @pl.when(n_state - j <= win - 1)
        def _(cols=cols):
            acc_ref[:, cols] += st_ref[0, :, cols]

    @pl.when(j == n_state - 1)
    def _():
        zpad = zz_ref.shape[0] - n_meta_pad
        zz_ref[:zpad, :] = jnp.zeros((zpad, zz_ref.shape[1]), F32)
        zz_ref[zpad:, :] = um_ref[...]
        t = lax.broadcasted_iota(jnp.int32, (n_meta_pad, 1), 0)
        for gi, win in enumerate(POOL_WINDOWS):
            cols = slice(gi * pg, (gi + 1) * pg)
            us = us_ref[:, cols]
            mix_s = (acc_ref[:, cols] + us) / float(win) - us
            wsum = zz_ref[zpad:, cols]
            for kk in range(1, win):
                wsum = wsum + zz_ref[zpad - kk:zpad - kk + n_meta_pad, cols]
            cnt = jnp.minimum(t + 1, win).astype(F32)
            mix_m = wsum / cnt - um_ref[:, cols]
            w = w_ref[gi]
            sc = sc_ref[:, cols]
            o_ref[:n_meta_pad, cols] = (jnp.dot(mix_m.astype(BF16), w, preferred_element_type=F32)
                                        * sc).astype(o_ref.dtype)
            o_ref[n_meta_pad:, cols] = (jnp.dot(mix_s.astype(BF16), w, preferred_element_type=F32)
                                        * sc).astype(o_ref.dtype)


def _conv3(x0, x1, x2, w_ref, b_ref):
    return b_ref[...] + x0 * w_ref[0:1, :] + x1 * w_ref[1:2, :] + x2 * w_ref[2:3, :]


def _ffn_up_kernel(x_ref, xm_ref, wg_ref, wv_ref, cwg_ref, cwv_ref, cbg_ref, cbv_ref,
                   s0g_ref, s0v_ref, s1g_ref, s1v_ref,
                   act_ref, tailg_ref, tailv_ref, sampg_ref, sampv_ref,
                   xs_ref, upg_a, upv_a, upg_b, upv_b, eg_ref, ev_ref, prevg_ref, prevv_ref,
                   *, bm, n_cols, n_meta, samp_lo, n_samp, tails, starts):
    i, j = pl.program_id(0), pl.program_id(1)
    last_tile = pl.num_programs(0) - 1

    @pl.when(j == 0)
    def _():
        xs_ref[:bm, :] = x_ref[...]
        xs_ref[bm:, :] = xm_ref[...]

    @pl.when((i == 0) & (j == 0))
    def _():
        prevg_ref[...] = jnp.zeros(prevg_ref.shape, F32)
        prevv_ref[...] = jnp.zeros(prevv_ref.shape, F32)

    k_chunk = xs_ref.shape[1] // N_K_CHUNKS
    groups = bm // 16
    piece_rows = [(groups // N_ROW_PIECES + (1 if p < groups % N_ROW_PIECES else 0)) * 16
                  for p in range(N_ROW_PIECES)]
    piece_start = [sum(piece_rows[:p]) for p in range(N_ROW_PIECES)]

    def product_chunk(c, acc):
        ks = slice(c * k_chunk, (c + 1) * k_chunk)
        xk = xs_ref[:, ks]
        pg = jnp.dot(xk, wg_ref[ks, :].astype(BF16), preferred_element_type=F32)
        pv = jnp.dot(xk, wv_ref[ks, :].astype(BF16), preferred_element_type=F32)
        return (pg, pv) if acc is None else (acc[0] + pg, acc[1] + pv)

    def finish_rows(p, up_g, up_v):
        jj = j - 1
        r0, n = piece_start[p], piece_rows[p]
        r1 = r0 + n

        def conv(up_ref, e_ref, prev_ref, cw_ref, cb_ref):
            if r0 == 0:
                e_ref[0:8, :] = prev_ref[jj]
            e_ref[8 + r0:8 + r1, :] = up_ref[r0:r1, :]
            if r1 == bm:
                prev_ref[jj] = up_ref[bm - 8:bm, :]
            return _conv3(e_ref[6 + r0:6 + r1, :], e_ref[7 + r0:7 + r1, :], up_ref[r0:r1, :], cw_ref, cb_ref)

        gate = conv(up_g, eg_ref, prevg_ref, cwg_ref, cbg_ref)
        val = conv(up_v, ev_ref, prevv_ref, cwv_ref, cbv_ref)
        act_ref[r0:r1, :] = (jax.nn.silu(gate) * val).astype(act_ref.dtype)

    def redo_sequence_start(lo, after_meta, up_g, up_v):
        def conv(up_ref, e_ref, cw_ref, cb_ref):
            if after_meta:
                e_ref[0:8, :] = up_ref[bm + n_meta - 8:bm + n_meta, :]
            else:
                e_ref[0:8, :] = jnp.zeros((8, e_ref.shape[1]), F32)
            e_ref[8:16, :] = up_ref[lo:lo + 8, :]
            return _conv3(e_ref[6:14, :], e_ref[7:15, :], up_ref[lo:lo + 8, :], cw_ref, cb_ref)

        gate = conv(up_g, eg_ref, cwg_ref, cbg_ref)
        val = conv(up_v, ev_ref, cwv_ref, cbv_ref)
        act_ref[lo:lo + 8, :] = (jax.nn.silu(gate) * val).astype(act_ref.dtype)

    def run(write, read):
        acc = None
        per = N_K_CHUNKS // N_ROW_PIECES
        for c in range(N_K_CHUNKS):
            if write is not None:
                acc = product_chunk(c, acc)
            if read is not None and c % per == 0:
                finish_rows(c // per, *read)
        if write is not None:
            write[0][...] = acc[0]
            write[1][...] = acc[1]
        if read is not None:
            finish_special_rows(*read)

    def finish_special_rows(up_g, up_v):
        for tile, lo, after_meta in starts:
            @pl.when(i == tile)
            def _(lo=lo, after_meta=after_meta):
                redo_sequence_start(lo, after_meta, up_g, up_v)

        samp = slice(samp_lo, samp_lo + n_samp)

        @pl.when(i == last_tile)
        def _():
            gate_s = _conv3(s0g_ref[...], s1g_ref[...], up_g[samp, :], cwg_ref, cbg_ref)
            val_s = _conv3(s0v_ref[...], s1v_ref[...], up_v[samp, :], cwv_ref, cbv_ref)
            act_ref[samp, :] = (jax.nn.silu(gate_s) * val_s).astype(act_ref.dtype)
            sampg_ref[0] = up_g[samp, :]
            sampv_ref[0] = up_v[samp, :]

        @pl.when(i != last_tile)
        def _():
            sampg_ref[0] = jnp.zeros(sampg_ref.shape[1:], F32)
            sampv_ref[0] = jnp.zeros(sampv_ref.shape[1:], F32)

        has_tail = i == tails[0][0]
        for tile, lo in tails:
            has_tail = has_tail | (i == tile)

            @pl.when(i == tile)
            def _(lo=lo):
                tailg_ref[0] = up_g[lo:lo + 8, :]
                tailv_ref[0] = up_v[lo:lo + 8, :]

        @pl.when(jnp.logical_not(has_tail))
        def _():
            tailg_ref[0] = jnp.zeros(tailg_ref.shape[1:], F32)
            tailv_ref[0] = jnp.zeros(tailv_ref.shape[1:], F32)

    buffers = ((upg_a, upv_a), (upg_b, upv_b))
    for parity in (0, 1):
        write, read = buffers[parity], buffers[1 - parity]
        is_mine = j % 2 == parity
        if parity == 0:
            @pl.when(j == 0)
            def _(write=write):
                run(write, None)

        @pl.when(is_mine & (j > 0) & (j < n_cols))
        def _(write=write, read=read):
            run(write, read)

        if n_cols % 2 == parity:
            @pl.when(j == n_cols)
            def _(read=read):
                run(None, read)


def kernel(x_prompt, x_sample, cache_ckv, cache_kpe, state_pool, state_conv, page_table, meta_tokens,
           norm_mix, w_in, g_qnorm, w_uq, g_kvnorm, w_uk, w_uv, w_attn_branch, w_pool_grp, pool_scale,
           w_pool_branch, w_out, norm_ffn, w_up, conv_w, conv_b, w_down, norm_final):
    nb, s_len, d = x_prompt.shape
    db, dec_t, _ = x_sample.shape
    depth = w_in.shape[0]
    assert depth == 1 and dec_t == 1
    nm = meta_tokens.shape[0]
    _, c_dim, n_heads, d_nope = w_uk.shape
    v_dim = w_uv.shape[3]
    r_dim = cache_kpe.shape[3]
    half = r_dim // 2
    q_lora = g_qnorm.shape[1]
    pw = pool_scale.shape[1]
    n_grp = len(POOL_WINDOWS)
    pg = pw // n_grp
    n_state = state_pool.shape[2]
    conv_taps = conv_w.shape[1]
    dff2 = w_up.shape[2]
    dff = dff2 // 2
    page_size = cache_ckv.shape[2]
    n_pages = page_table.shape[1]
    past = n_pages * page_size
    scale = float(d_nope + r_dim) ** -0.5 * LOG2_E
    hd = 2 * LANES
    assert d_nope == LANES and v_dim == LANES and r_dim <= LANES and conv_taps == 3
    assert nm >= max(POOL_WINDOWS) - 1 and nm % 16 == 0 and nm <= META_PAD
    assert n_state == max(POOL_WINDOWS) - 1

    rt = nb * s_len
    r_meta = rt
    r_samp = rt + META_PAD
    e_rows = META_PAD + db
    m_rows = rt + e_rows
    assert rt % META_PAD == 0 and r_samp % db == 0 and db % 16 == 0 and r_meta % nm == 0

    off_q, off_kv, off_kr, off_gp = pw, pw + q_lora, pw + q_lora + c_dim, pw + q_lora + c_dim + r_dim
    assert off_q % q_lora == 0 and off_kv % c_dim == 0 and off_kr % LANES == 0 and pw % pg == 0

    n_a = off_kr + LANES
    n_a_pad = -(-n_a // 256) * 256
    w_a = jnp.concatenate([w_in[0, :, :off_gp], jnp.zeros((d, n_a_pad - off_gp), F32)], axis=1).astype(BF16)
    w_in_t = jnp.swapaxes(w_in[0], 0, 1)
    wq = w_uq[0].reshape(q_lora, n_heads, d_nope + r_dim)
    wq = jnp.concatenate([wq, jnp.zeros((q_lora, n_heads, hd - d_nope - r_dim), F32)], axis=2)
    wq = wq.reshape(q_lora, n_heads * hd).astype(BF16)
    w_uk2 = w_uk[0].reshape(c_dim, n_heads * d_nope).astype(BF16)
    w_ukt = jnp.transpose(w_uk[0], (1, 2, 0)).astype(BF16)
    w_uv2 = w_uv[0].reshape(c_dim, n_heads * v_dim).astype(BF16)
    w_pg = w_pool_grp[0].astype(BF16)
    w_d = w_down[0].astype(BF16)
    w_ab, w_pb, w_o, w_u = w_attn_branch[0], w_pool_branch[0], w_out[0], w_up[0]

    pos = jnp.concatenate([
        jnp.tile(nm + jnp.arange(s_len), nb),
        jnp.arange(nm), jnp.zeros((META_PAD - nm,), jnp.int32),
        jnp.full((db,), past, jnp.int32)])
    inv = ROPE_BASE ** (-jnp.arange(0, r_dim, 2, dtype=F32) / r_dim)
    ang = pos.astype(F32)[:, None] * inv[None, :]
    cos, sin = jnp.cos(ang), jnp.sin(ang)
    zl = jnp.zeros((m_rows, LANES - 2 * half), F32)
    zh = jnp.zeros((m_rows, half), F32)
    tab_c = jnp.concatenate([cos, cos, zl], axis=1)
    tab_s1 = jnp.concatenate([-sin, zh, zl], axis=1)
    tab_s2 = jnp.concatenate([zh, sin, zl], axis=1)

    bm = _pick_block(m_rows, 1100, 16)
    row_tab_spec = pl.BlockSpec((bm, LANES), lambda i, j: (i, 0))

    x_extra = jnp.concatenate([meta_tokens.astype(F32), jnp.zeros((META_PAD - nm, d), F32),
                               x_sample.reshape(db, d)], axis=0)
    rb = _pick_block(e_rows, 256, 16)
    assert rt % rb == 0
    x_all, h = _norm_in(x_prompt.reshape(rt, d), x_extra, norm_mix[0], rb)

    bn_a = _pick_block(n_a_pad, 768, LANES)
    u1 = _matmul(h, w_a, m=m_rows, k=d, n=n_a_pad, bm=bm, bn=bn_a, out_cols=n_a_pad, out_bn=bn_a,
                 out_dtype=F32, name="in_proj")

    def gate_epilogue(acc, o_ref):
        o_ref[...] = jax.nn.sigmoid(acc).astype(o_ref.dtype)

    bn_d = _pick_block(d, 512, LANES)
    gates = _matmul(h, w_in_t, m=m_rows, k=d, n=2 * d, bm=bm, bn=bn_d, out_cols=2 * d, out_bn=bn_d,
                    out_dtype=BF16, name="gate_proj", epilogue=gate_epilogue, w_nk_row0=off_gp)

    def q_prologue(x_ref, g_ref):
        return _rms(x_ref[...], g_ref[...])

    bn_q = _pick_block(n_heads * hd, 1024, hd)

    def q_epilogue(acc, o_ref, ct_ref, s1_ref, s2_ref):
        acc = acc * scale
        for cb in range(bn_q // LANES):
            cols = slice(cb * LANES, (cb + 1) * LANES)
            v = acc[:, cols]
            if cb % 2 == 1:
                v = _rope_tile(v, ct_ref[...], s1_ref[...], s2_ref[...], half)
            o_ref[:, cols] = v.astype(o_ref.dtype)

    q_all = _matmul(u1, wq, m=m_rows, k=q_lora, n=n_heads * hd, bm=bm, bn=bn_q,
                    out_cols=n_heads * hd, out_bn=bn_q, out_dtype=BF16, name="q_proj",
                    x_col_block=off_q // q_lora,
                    pro=(g_qnorm[0].reshape(1, q_lora),), pro_specs=(pl.BlockSpec((1, q_lora), lambda i, j: (0, 0)),),
                    prologue=q_prologue,
                    extra=(tab_c, tab_s1, tab_s2), extra_specs=(row_tab_spec,) * 3, epilogue=q_epilogue)

    tab1 = pl.BlockSpec((bm, LANES), lambda i: (i, 0))
    ckv32, ckv16, kpe32 = pl.pallas_call(
        functools.partial(_kv_prep_kernel, half=half),
        grid=(m_rows // bm,),
        in_specs=[pl.BlockSpec((bm, c_dim), lambda i: (i, off_kv // c_dim)),
                  pl.BlockSpec((bm, LANES), lambda i: (i, off_kr // LANES)),
                  pl.BlockSpec((1, c_dim), lambda i: (0, 0)),
                  tab1, tab1, tab1],
        out_specs=[pl.BlockSpec((bm, c_dim), lambda i: (i, 0)),
                   pl.BlockSpec((bm, c_dim), lambda i: (i, 0)),
                   pl.BlockSpec((bm, LANES), lambda i: (i, 0))],
        out_shape=[jax.ShapeDtypeStruct((m_rows, c_dim), F32),
                   jax.ShapeDtypeStruct((m_rows, c_dim), BF16),
                   jax.ShapeDtypeStruct((m_rows, LANES), F32)],
        compiler_params=_params("arbitrary"),
        name="kv_prep",
    )(u1, u1, g_kvnorm[0].reshape(1, c_dim), tab_c, tab_s1, tab_s2)

    hps = _pick_block(n_heads, 16, 1)

    def k_epilogue(acc, o_ref, kpe_ref):
        kr = kpe_ref[...].astype(o_ref.dtype)
        for hh in range(hps):
            o_ref[:, hh * hd:hh * hd + d_nope] = acc[:, hh * d_nope:(hh + 1) * d_nope].astype(o_ref.dtype)
            o_ref[:, hh * hd + d_nope:(hh + 1) * hd] = kr

    k_all = _matmul(ckv16, w_uk2, m=m_rows, k=c_dim, n=n_heads * d_nope, bm=bm, bn=hps * d_nope,
                    out_cols=n_heads * hd, out_bn=hps * hd, out_dtype=BF16, name="k_up",
                    extra=(kpe32,), extra_specs=(row_tab_spec,), epilogue=k_epilogue)
    bn_v = _pick_block(n_heads * v_dim, 2048, LANES)
    v_all = _matmul(ckv16, w_uv2, m=m_rows, k=c_dim, n=n_heads * v_dim, bm=bm, bn=bn_v,
                    out_cols=n_heads * v_dim, out_bn=bn_v, out_dtype=BF16, name="v_up")

    bq = _pick_block(s_len, 512, 128)
    mb = r_meta // META_PAD
    assert e_rows <= s_len

    def prompt_block(b, col, col_off=0):
        return jnp.minimum(b, nb - 1), jnp.where(b < nb, col, 0) + col_off
    o_all = pl.pallas_call(
        functools.partial(_attn_tok_kernel, bq=bq, n_meta=nm, n_prompts=nb),
        grid=(nb + 1, n_heads),
        in_specs=[pl.BlockSpec((s_len, hd), prompt_block),
                  pl.BlockSpec((s_len, hd), prompt_block),
                  pl.BlockSpec((s_len, v_dim), prompt_block),
                  pl.BlockSpec((META_PAD, hd), lambda b, hh: (mb, hh)),
                  pl.BlockSpec((META_PAD, v_dim), lambda b, hh: (mb, hh))],
        out_specs=pl.BlockSpec((s_len, v_dim), lambda b, hh: (b, hh)),
        out_shape=jax.ShapeDtypeStruct((m_rows, n_heads * v_dim), BF16),
        compiler_params=_params("parallel", "arbitrary"),
        name="attn_prompt",
    )(q_all, k_all, v_all, k_all, v_all)

    o_all = pl.pallas_call(
        functools.partial(_attn_meta_kernel, n_meta=nm),
        grid=(n_heads,),
        in_specs=[pl.BlockSpec((META_PAD, hd), lambda hh: (mb, hh)),
                  pl.BlockSpec((META_PAD, hd), lambda hh: (mb, hh)),
                  pl.BlockSpec((META_PAD, v_dim), lambda hh: (mb, hh)),
                  pl.BlockSpec(memory_space=pl.ANY)],
        out_specs=pl.BlockSpec((META_PAD, v_dim), lambda hh: (mb, hh)),
        out_shape=jax.ShapeDtypeStruct((m_rows, n_heads * v_dim), BF16),
        input_output_aliases={3: 0},
        compiler_params=_params("arbitrary"),
        name="attn_meta",
    )(q_all, k_all, v_all, o_all)

    sb = r_samp // db
    q_lat = pl.pallas_call(
        _qlat_kernel,
        grid=(n_heads,),
        in_specs=[pl.BlockSpec((db, d_nope), lambda hh: (sb, hh * (hd // d_nope))),
                  pl.BlockSpec((1, d_nope, c_dim), lambda hh: (hh, 0, 0))],
        out_specs=pl.BlockSpec((1, db, c_dim), lambda hh: (hh, 0, 0)),
        out_shape=jax.ShapeDtypeStruct((n_heads, db, c_dim), F32),
        compiler_params=_params("arbitrary"),
        name="q_absorb",
    )(q_all, w_ukt)
    qk_dim = -(-(c_dim + r_dim) // LANES) * LANES
    q_samp = q_all[r_samp:r_samp + db].reshape(db, n_heads, hd)
    q_cat = jnp.concatenate([jnp.transpose(q_lat, (1, 0, 2)).astype(BF16),
                             q_samp[:, :, d_nope:d_nope + r_dim],
                             jnp.zeros((db, n_heads, qk_dim - c_dim - r_dim), BF16)], axis=2)
    new_cat = jnp.concatenate([ckv32[r_samp:r_samp + db], kpe32[r_samp:r_samp + db, :r_dim],
                               jnp.zeros((db, qk_dim - c_dim - r_dim), F32)], axis=1).reshape(db, 1, qk_dim)
    pages = _pick_block(n_pages, 32, 1)
    streams = _pick_block(pages, 4, 1)
    cache_kpe_t = jnp.swapaxes(cache_kpe[0], 1, 2)
    o_lat = _decode_attention(page_table, cache_ckv[0], cache_kpe_t, q_cat, new_cat, pages=pages, streams=streams)
    o_lat_t = jnp.transpose(o_lat, (1, 0, 2)).astype(BF16)
    o_all = pl.pallas_call(
        _ouv_kernel,
        grid=(n_heads,),
        in_specs=[pl.BlockSpec((1, db, c_dim), lambda hh: (hh, 0, 0)),
                  pl.BlockSpec((c_dim, v_dim), lambda hh: (0, hh)),
                  pl.BlockSpec(memory_space=pl.ANY)],
        out_specs=pl.BlockSpec((db, v_dim), lambda hh: (sb, hh)),
        out_shape=jax.ShapeDtypeStruct((m_rows, n_heads * v_dim), BF16),
        input_output_aliases={2: 0},
        compiler_params=_params("arbitrary"),
        name="o_up_decode",
    )(o_lat_t, w_uv2, o_all)

    scale_row = pool_scale[0].reshape(1, pw)
    pm = pl.pallas_call(
        functools.partial(_pool_tok_kernel, n_meta=nm, n_prompts=nb),
        grid=(nb + 1, n_grp),
        in_specs=[pl.BlockSpec((s_len, pg), prompt_block),
                  pl.BlockSpec((nm, pg), lambda b, g: (r_meta // nm, g)),
                  pl.BlockSpec((1, pg, pg), lambda b, g: (g, 0, 0)),
                  pl.BlockSpec((1, pg), lambda b, g: (0, g))],
        out_specs=pl.BlockSpec((s_len, pg), lambda b, g: (b, g)),
        out_shape=jax.ShapeDtypeStruct((m_rows, pw), BF16),
        scratch_shapes=[pltpu.VMEM((POOL_ZERO_ROWS + nm + s_len, pg), F32),
                        pltpu.VMEM((POOL_ZERO_ROWS + nm + s_len, pg), F32)],
        compiler_params=_params("parallel", "arbitrary"),
        name="pool_prompt",
    )(u1, u1, w_pg, scale_row)
    st_rows = jnp.transpose(state_pool[0], (1, 0, 2))
    zrows = POOL_ZERO_ROWS
    pm = pl.pallas_call(
        functools.partial(_pool_extra_kernel, n_state=n_state, pg=pg, n_meta_pad=META_PAD),
        grid=(n_state,),
        in_specs=[pl.BlockSpec((1, db, pw), lambda j: (j, 0, 0)),
                  pl.BlockSpec((db, pw), lambda j: (sb, 0)),
                  pl.BlockSpec((META_PAD, pw), lambda j: (mb, 0)),
                  pl.BlockSpec((n_grp, pg, pg), lambda j: (0, 0, 0)),
                  pl.BlockSpec((1, pw), lambda j: (0, 0)),
                  pl.BlockSpec(memory_space=pl.ANY)],
        out_specs=pl.BlockSpec((e_rows, pw), lambda j: (rt // e_rows, 0)),
        out_shape=jax.ShapeDtypeStruct((m_rows, pw), BF16),
        scratch_shapes=[pltpu.VMEM((db, pw), F32), pltpu.VMEM((zrows + META_PAD, pw), F32)],
        input_output_aliases={5: 0},
        compiler_params=_params("arbitrary"),
        name="pool_extra",
    )(st_rows, u1, u1, w_pg, scale_row, pm)

    def pool_branch_epilogue(acc, o_ref, g_ref):
        o_ref[...] = (g_ref[...].astype(F32) * acc).astype(o_ref.dtype)

    tile_spec = pl.BlockSpec((bm, bn_d), lambda i, j: (i, j))
    t1 = _matmul(pm, w_pb, m=m_rows, k=pw, n=d, bm=bm, bn=bn_d, out_cols=d, out_bn=bn_d, out_dtype=BF16,
                 name="pool_branch", extra=(gates,), extra_specs=(tile_spec,), epilogue=pool_branch_epilogue)

    def attn_branch_epilogue(acc, o_ref, g_ref, t_ref):
        o_ref[...] = (t_ref[...].astype(F32) + g_ref[...].astype(F32) * acc).astype(o_ref.dtype)

    mixed = _matmul(o_all, w_ab, m=m_rows, k=n_heads * v_dim, n=d, bm=bm, bn=bn_d, out_cols=d, out_bn=bn_d,
                    out_dtype=BF16, name="attn_branch", extra=(gates, t1),
                    extra_specs=(pl.BlockSpec((bm, bn_d), lambda i, j: (i, j + d // bn_d)), tile_spec),
                    epilogue=attn_branch_epilogue)

    def residual_epilogue(acc, o_ref, r_ref):
        o_ref[...] = r_ref[...] + acc

    x1 = _matmul(mixed, w_o, m=m_rows, k=d, n=d, bm=bm, bn=bn_d, out_cols=d, out_bn=bn_d, out_dtype=F32,
                 name="out_proj", extra=(x_all,), extra_specs=(tile_spec,), epilogue=residual_epilogue)

    rb2 = _pick_block(m_rows, 512, 16)
    h2 = _norm_rows(x1, norm_ffn[0], row0=0, nrows=m_rows, rb=rb2, out_dtype=BF16, name="norm_ffn")
    bc = _pick_block(dff, 256, LANES)
    ncb = dff // bc
    n_mt = m_rows // bm
    last_mt = n_mt - 1
    samp_lo = r_samp - last_mt * bm
    assert samp_lo >= 0 and r_samp + db == m_rows
    tails = tuple(divmod((b + 1) * s_len - 8, bm) for b in range(nb))
    assert len({t for t, _ in tails}) == nb and all(lo + 8 <= bm for _, lo in tails)
    starts = tuple((*divmod(b * s_len, bm), True) for b in range(nb)) + ((*divmod(rt, bm), False),)
    assert nm >= 8 and all(lo % 8 == 0 for _, lo, _ in starts)
    cw, cb = conv_w[0], conv_b[0].reshape(1, dff2)
    sc2d = state_conv[0].reshape(db, (conv_taps - 1) * dff2)

    def done_block(j):
        return jnp.maximum(j - 1, 0)

    def state_spec(off):
        return pl.BlockSpec((db, bc), lambda i, j: (0, jnp.where(i == last_mt, done_block(j), 0) + off))

    samp_spec = pl.BlockSpec((1, db, bc), lambda i, j: (jnp.where(i == last_mt, last_mt + done_block(j), i), 0, 0))
    tail_spec = pl.BlockSpec((1, 8, bc), lambda i, j: (i, 0, done_block(j)))
    act, tail_g, tail_v, samp_g, samp_v = pl.pallas_call(
        functools.partial(_ffn_up_kernel, bm=bm, n_cols=ncb, n_meta=nm,
                          samp_lo=samp_lo, n_samp=db, tails=tails, starts=starts),
        grid=(n_mt, ncb + 1),
        in_specs=[pl.BlockSpec((bm, d), lambda i, j: (i, 0), pipeline_mode=pl.Buffered(1)),
                  pl.BlockSpec((nm, d), lambda i, j: (r_meta // nm, 0)),
                  pl.BlockSpec((d, bc), lambda i, j: (0, jnp.minimum(j, ncb - 1))),
                  pl.BlockSpec((d, bc), lambda i, j: (0, jnp.minimum(j, ncb - 1) + ncb)),
                  pl.BlockSpec((conv_taps, bc), lambda i, j: (0, done_block(j))),
                  pl.BlockSpec((conv_taps, bc), lambda i, j: (0, done_block(j) + ncb)),
                  pl.BlockSpec((1, bc), lambda i, j: (0, done_block(j))),
                  pl.BlockSpec((1, bc), lambda i, j: (0, done_block(j) + ncb)),
                  state_spec(0), state_spec(ncb), state_spec(2 * ncb), state_spec(3 * ncb)],
        out_specs=[pl.BlockSpec((bm, bc), lambda i, j: (i, done_block(j))),
                   tail_spec, tail_spec, samp_spec, samp_spec],
        out_shape=[jax.ShapeDtypeStruct((m_rows, dff), BF16),
                   jax.ShapeDtypeStruct((n_mt, 8, dff), F32),
                   jax.ShapeDtypeStruct((n_mt, 8, dff), F32),
                   jax.ShapeDtypeStruct((last_mt + ncb, db, bc), F32),
                   jax.ShapeDtypeStruct((last_mt + ncb, db, bc), F32)],
        scratch_shapes=[pltpu.VMEM((bm + nm, d), BF16),
                        pltpu.VMEM((bm + nm, bc), F32), pltpu.VMEM((bm + nm, bc), F32),
                        pltpu.VMEM((bm + nm, bc), F32), pltpu.VMEM((bm + nm, bc), F32),
                        pltpu.VMEM((8 + bm, bc), F32), pltpu.VMEM((8 + bm, bc), F32),
                        pltpu.VMEM((ncb, 8, bc), F32), pltpu.VMEM((ncb, 8, bc), F32)],
        compiler_params=_params("arbitrary", "arbitrary"),
        name="ffn_up",
    )(h2, h2, w_u, w_u, cw, cw, cb, cb, sc2d, sc2d, sc2d, sc2d)
    n_tail = conv_taps - 1
    tail_tiles = jnp.array([t for t, _ in tails])
    up_tail = jnp.concatenate([tail_g[tail_tiles, 8 - n_tail:], tail_v[tail_tiles, 8 - n_tail:]], axis=2)

    def sample_cols(a):
        return jnp.transpose(a[last_mt:], (1, 0, 2)).reshape(db, dff)

    up_samp = jnp.concatenate([sample_cols(samp_g), sample_cols(samp_v)], axis=1)

    bm_d = _pick_block(m_rows, 528, 16)
    bn_dn = _pick_block(d, 512, LANES)
    x2 = _matmul(act, w_d, m=m_rows, k=dff, n=d, bm=bm_d, bn=bn_dn, out_cols=d, out_bn=bn_dn, out_dtype=F32,
                 name="ffn_down", extra=(x1,), extra_specs=(pl.BlockSpec((bm_d, bn_dn), lambda i, j: (i, j)),),
                 epilogue=residual_epilogue)

    rb3 = _pick_block(rt, 512, 16)
    y_prompt = _norm_rows(x2, norm_final, row0=0, nrows=rt, rb=rb3, out_dtype=F32, name="norm_final_prompt")
    y_sample = _norm_rows(x2, norm_final, row0=r_samp, nrows=db, rb=db, out_dtype=F32, name="norm_final_sample")

    def with_meta(a, width):
        tok = a[:rt, :width].reshape(nb, s_len, width)
        meta = jnp.broadcast_to(a[r_meta:r_meta + nm, :width][None], (nb, nm, width))
        return jnp.concatenate([meta, tok], axis=1)[None]

    def prompt_tail(a, n_tail, width):
        return jnp.stack([a[(b + 1) * s_len - n_tail:(b + 1) * s_len, :width] for b in range(nb)])[None]

    return (
        y_prompt.reshape(nb, s_len, d),
        y_sample.reshape(db, 1, d),
        with_meta(ckv32, c_dim),
        with_meta(kpe32, r_dim),
        prompt_tail(u1, n_state, pw),
        up_tail[None],
        ckv32[r_samp:r_samp + db].reshape(1, db, 1, c_dim),
        kpe32[r_samp:r_samp + db, :r_dim].reshape(1, db, 1, r_dim),
        jnp.concatenate([state_pool[0, :, 1:], u1[r_samp:r_samp + db, :pw][:, None]], axis=1)[None],
        jnp.concatenate([state_conv[0, :, 1:], up_samp[:, None]], axis=1)[None],
    )
```

```python
import functools

import jax
import jax.numpy as jnp
from jax import lax
from jax.experimental import pallas as pl
from jax.experimental.pallas import tpu as pltpu

EPS = 1e-6
ROPE_BASE = 10000.0
POOL_WINDOWS = (2, 4, 8, 16)
LANES = 128
META_PAD = 128
POOL_ZERO_ROWS = 16
MASK_VALUE = -1e30
LOG2_E = 1.4426950408889634
N_K_CHUNKS = 16
N_ROW_PIECES = 8
VMEM_LIMIT_BYTES = 60 * 1024 * 1024

F32 = jnp.float32
BF16 = jnp.bfloat16


def _params(*sem):
    return pltpu.CompilerParams(dimension_semantics=sem, vmem_limit_bytes=VMEM_LIMIT_BYTES)


def _pick_block(n, cap, mult):
    best = None
    for d in range(mult, min(n, cap) + 1, mult):
        if n % d == 0:
            best = d
    assert best is not None, (n, cap, mult)
    return best


def _rms(x, g):
    return x * lax.rsqrt(jnp.mean(x * x, axis=-1, keepdims=True) + EPS) * g


def _norm_in_kernel(xt_ref, xe_ref, g_ref, x_ref, h_ref, *, n_tok_blocks):
    i = pl.program_id(0)

    def emit(src):
        x = src[...]
        x_ref[...] = x
        h_ref[...] = _rms(x, g_ref[...]).astype(h_ref.dtype)

    @pl.when(i < n_tok_blocks)
    def _():
        emit(xt_ref)

    @pl.when(i >= n_tok_blocks)
    def _():
        emit(xe_ref)


def _norm_in(x_tok, x_extra, g, rb):
    rt, d = x_tok.shape
    e = x_extra.shape[0]
    m = rt + e
    ntb, neb = rt // rb, e // rb
    return pl.pallas_call(
        functools.partial(_norm_in_kernel, n_tok_blocks=ntb),
        grid=(ntb + neb,),
        in_specs=[
            pl.BlockSpec((rb, d), lambda i: (jnp.minimum(i, ntb - 1), 0)),
            pl.BlockSpec((rb, d), lambda i: (jnp.maximum(i - ntb, 0), 0)),
            pl.BlockSpec((1, d), lambda i: (0, 0)),
        ],
        out_specs=[pl.BlockSpec((rb, d), lambda i: (i, 0)),
                   pl.BlockSpec((rb, d), lambda i: (i, 0))],
        out_shape=[jax.ShapeDtypeStruct((m, d), F32), jax.ShapeDtypeStruct((m, d), BF16)],
        compiler_params=_params("arbitrary"),
        name="norm_in",
    )(x_tok, x_extra, g.reshape(1, d))


def _norm_rows_kernel(x_ref, g_ref, o_ref):
    o_ref[...] = _rms(x_ref[...], g_ref[...]).astype(o_ref.dtype)


def _norm_rows(x, g, *, row0, nrows, rb, out_dtype, name):
    d = x.shape[1]
    assert row0 % rb == 0 and nrows % rb == 0
    b0 = row0 // rb
    return pl.pallas_call(
        _norm_rows_kernel,
        grid=(nrows // rb,),
        in_specs=[pl.BlockSpec((rb, d), lambda i: (b0 + i, 0)),
                  pl.BlockSpec((1, d), lambda i: (0, 0))],
        out_specs=pl.BlockSpec((rb, d), lambda i: (i, 0)),
        out_shape=jax.ShapeDtypeStruct((nrows, d), out_dtype),
        compiler_params=_params("arbitrary"),
        name=name,
    )(x, g.reshape(1, d))


def _mm_kernel(*refs, n_pro, n_extra, prologue, epilogue, w_is_nk=False):
    x_ref, w_ref = refs[0], refs[1]
    pro_refs = refs[2:2 + n_pro]
    extra_refs = refs[2 + n_pro:2 + n_pro + n_extra]
    o_ref = refs[2 + n_pro + n_extra]
    if prologue is not None:
        xs_ref = refs[3 + n_pro + n_extra]

        @pl.when(pl.program_id(1) == 0)
        def _():
            xs_ref[...] = prologue(x_ref, *pro_refs).astype(xs_ref.dtype)

        xv = xs_ref[...]
    else:
        xv = x_ref[...]
    contract = (((1,), (1,)), ((), ())) if w_is_nk else (((1,), (0,)), ((), ()))
    acc = lax.dot_general(xv, w_ref[...].astype(BF16), contract, preferred_element_type=F32)
    epilogue(acc, o_ref, *extra_refs)


def _store_epilogue(acc, o_ref):
    o_ref[...] = acc.astype(o_ref.dtype)


def _matmul(x, w, *, m, k, n, bm, bn, out_cols, out_bn, out_dtype, name,
            x_col_block=0, pro=(), pro_specs=(), prologue=None,
            extra=(), extra_specs=(), epilogue=_store_epilogue, w_nk_row0=None):
    assert m % bm == 0 and n % bn == 0
    scratch = [pltpu.VMEM((bm, k), BF16)] if prologue is not None else []
    kern = functools.partial(_mm_kernel, n_pro=len(pro), n_extra=len(extra),
                             prologue=prologue, epilogue=epilogue, w_is_nk=w_nk_row0 is not None)
    if w_nk_row0 is None:
        w_spec = pl.BlockSpec((k, bn), lambda i, j: (0, j))
    else:
        assert w_nk_row0 % 8 == 0
        w_spec = pl.BlockSpec((pl.Element(bn), pl.Element(k)),
                              lambda i, j: (pl.multiple_of(w_nk_row0 + j * bn, 8), 0))
    return pl.pallas_call(
        kern,
        grid=(m // bm, n // bn),
        in_specs=[pl.BlockSpec((bm, k), lambda i, j: (i, x_col_block)),
                  w_spec,
                  *pro_specs, *extra_specs],
        out_specs=pl.BlockSpec((bm, out_bn), lambda i, j: (i, j)),
        out_shape=jax.ShapeDtypeStruct((m, out_cols), out_dtype),
        scratch_shapes=scratch,
        compiler_params=_params("parallel", "arbitrary"),
        name=name,
    )(x, w, *pro, *extra)


def _rope_tile(v, ct, s1, s2, half):
    return v * ct + pltpu.roll(v, LANES - half, 1) * s1 + pltpu.roll(v, half, 1) * s2


def _kv_prep_kernel(ukv_ref, ukr_ref, g_ref, ct_ref, s1_ref, s2_ref, c32_ref, c16_ref, kpe_ref, *, half):
    c = _rms(ukv_ref[...], g_ref[...])
    c32_ref[...] = c
    c16_ref[...] = c.astype(BF16)
    kpe_ref[...] = _rope_tile(ukr_ref[...], ct_ref[...], s1_ref[...], s2_ref[...], half)


def _softmax_step(q, k, v, carry, mask=None):
    m_i, l_i, acc = carry
    s = lax.dot_general(q, k, (((1,), (1,)), ((), ())), preferred_element_type=F32)
    if mask is not None:
        s = jnp.where(mask, s, MASK_VALUE)
    m_new = jnp.maximum(m_i, jnp.max(s, axis=1, keepdims=True))
    alpha = jnp.exp2(m_i - m_new)
    p = jnp.exp2(s - m_new)
    l_new = alpha * l_i + jnp.sum(p, axis=1, keepdims=True)
    acc_new = alpha * acc + jnp.dot(p.astype(v.dtype), v, preferred_element_type=F32)
    return m_new, l_new, acc_new


def _zero_fill_extra_rows(o_ref, is_extra_step):
    @pl.when(is_extra_step)
    def _():
        o_ref[...] = jnp.zeros(o_ref.shape, o_ref.dtype)


def _attn_tok_kernel(q_ref, k_ref, v_ref, km_ref, vm_ref, o_ref, *, bq, n_meta, n_prompts):
    s_len, vdim = v_ref.shape
    _zero_fill_extra_rows(o_ref, pl.program_id(0) == n_prompts)

    @pl.when(pl.program_id(0) < n_prompts)
    def _():
        meta_mask = lax.broadcasted_iota(jnp.int32, (bq, km_ref.shape[0]), 1) < n_meta
        causal = (lax.broadcasted_iota(jnp.int32, (bq, bq), 1) <= lax.broadcasted_iota(jnp.int32, (bq, bq), 0))
        for qi in range(s_len // bq):
            rows = slice(qi * bq, (qi + 1) * bq)
            q = q_ref[rows, :]
            carry = (jnp.full((bq, 1), MASK_VALUE, F32), jnp.zeros((bq, 1), F32), jnp.zeros((bq, vdim), F32))
            carry = _softmax_step(q, km_ref[...], vm_ref[...], carry, meta_mask)
            for kj in range(qi):
                keys = slice(kj * bq, (kj + 1) * bq)
                carry = _softmax_step(q, k_ref[keys, :], v_ref[keys, :], carry)
            carry = _softmax_step(q, k_ref[rows, :], v_ref[rows, :], carry, causal)
            _, l_i, acc = carry
            o_ref[rows, :] = (acc / l_i).astype(o_ref.dtype)


def _attn_meta_kernel(q_ref, km_ref, vm_ref, o_any, o_ref, *, n_meta):
    del o_any
    n = q_ref.shape[0]
    row = lax.broadcasted_iota(jnp.int32, (n, n), 0)
    col = lax.broadcasted_iota(jnp.int32, (n, n), 1)
    mask = (col <= row) & (col < n_meta)
    carry = (jnp.full((n, 1), MASK_VALUE, F32), jnp.zeros((n, 1), F32),
             jnp.zeros((n, vm_ref.shape[1]), F32))
    _, l_i, acc = _softmax_step(q_ref[...], km_ref[...], vm_ref[...], carry, mask)
    o_ref[...] = (acc / l_i).astype(o_ref.dtype)


def _decode_kernel(pt_ref, ckv_hbm, kpe_hbm, q_ref, new_ref, o_ref,
                   ckv_buf, kpe_buf, sem, kc_ref, kp_ref, m_ref, l_ref, acc_ref,
                   *, pages, streams, page_size, c_dim, r_dim):
    b, j = pl.program_id(0), pl.program_id(1)
    n_b, n_j = pl.num_programs(0), pl.num_programs(1)
    step = b * n_j + j
    slot = step % 2
    per = pages // streams

    def page_copies(bb, jj, sl):
        copies = []
        for p in range(pages):
            page = pt_ref[bb, jj * pages + p]
            copies.append(pltpu.make_async_copy(ckv_hbm.at[page], ckv_buf.at[sl, p], sem.at[sl, 0]))
            copies.append(pltpu.make_async_copy(kpe_hbm.at[page], kpe_buf.at[sl, p], sem.at[sl, 1]))
        return copies

    @pl.when(step == 0)
    def _():
        for cp in page_copies(b, j, slot):
            cp.start()

    @pl.when(step + 1 < n_b * n_j)
    def _():
        wrap = j + 1 == n_j
        for cp in page_copies(jnp.where(wrap, b + 1, b), jnp.where(wrap, 0, j + 1), 1 - slot):
            cp.start()

    for cp in page_copies(b, j, slot):
        cp.wait()

    q = q_ref[0]
    q_lat, q_pe = q[:, :c_dim], q[:, c_dim:c_dim + r_dim]

    @pl.when(j == 0)
    def _():
        new = new_ref[0]
        s_self = jnp.sum(q.astype(F32) * new, axis=1, keepdims=True)
        m_ref[0] = s_self
        l_ref[0] = jnp.ones_like(s_self)
        acc_ref[0] = jnp.broadcast_to(new[:, :c_dim], acc_ref.shape[1:])
        for st in range(1, streams):
            m_ref[st] = jnp.full(m_ref.shape[1:], MASK_VALUE, F32)
            l_ref[st] = jnp.zeros(l_ref.shape[1:], F32)
            acc_ref[st] = jnp.zeros(acc_ref.shape[1:], F32)

    def scores(st):
        for p in range(per):
            keys = slice(p * page_size, (p + 1) * page_size)
            kc_ref[st, keys, :] = ckv_buf[slot, st * per + p].astype(kc_ref.dtype)
            kp_ref[st, :, keys] = kpe_buf[slot, st * per + p].astype(kp_ref.dtype)
        return (lax.dot_general(q_lat, kc_ref[st], (((1,), (1,)), ((), ())), preferred_element_type=F32)
                + jnp.dot(q_pe, kp_ref[st], preferred_element_type=F32))

    def softmax(st, s):
        m_old = m_ref[st]
        m_new = jnp.maximum(m_old, jnp.max(s, axis=1, keepdims=True))
        alpha = jnp.exp2(m_old - m_new)
        p_att = jnp.exp2(s - m_new)
        l_ref[st] = alpha * l_ref[st] + jnp.sum(p_att, axis=1, keepdims=True)
        m_ref[st] = m_new
        return alpha, p_att.astype(kc_ref.dtype)

    def values(st, alpha, p_att):
        acc_ref[st] = alpha * acc_ref[st] + jnp.dot(p_att, kc_ref[st], preferred_element_type=F32)

    pending_s, pending_p = None, None
    for st in range(streams):
        s = scores(st)
        if pending_p is not None:
            values(st - 2, *pending_p)
            pending_p = None
        if pending_s is not None:
            pending_p = softmax(st - 1, pending_s)
        pending_s = s
    if pending_p is not None:
        values(streams - 2, *pending_p)
    values(streams - 1, *softmax(streams - 1, pending_s))

    @pl.when(j == pl.num_programs(1) - 1)
    def _():
        m_all = m_ref[0]
        for st in range(1, streams):
            m_all = jnp.maximum(m_all, m_ref[st])
        l_all = jnp.zeros(l_ref.shape[1:], F32)
        acc_all = jnp.zeros(acc_ref.shape[1:], F32)
        for st in range(streams):
            w = jnp.exp2(m_ref[st] - m_all)
            l_all = l_all + w * l_ref[st]
            acc_all = acc_all + w * acc_ref[st]
        o_ref[0] = (acc_all / l_all).astype(o_ref.dtype)


def _decode_attention(page_table, cache_ckv, cache_kpe_t, q_cat, new_cat, *, pages, streams):
    db, n_pages = page_table.shape
    _, page_size, c_dim = cache_ckv.shape
    r_dim = cache_kpe_t.shape[1]
    n_heads, qk_dim = q_cat.shape[1], q_cat.shape[2]
    assert n_pages % pages == 0 and pages % streams == 0
    per = pages // streams

    grid_spec = pltpu.PrefetchScalarGridSpec(
        num_scalar_prefetch=1,
        grid=(db, n_pages // pages),
        in_specs=[pl.BlockSpec(memory_space=pl.ANY),
                  pl.BlockSpec(memory_space=pl.ANY),
                  pl.BlockSpec((1, n_heads, qk_dim), lambda b, j, pt: (b, 0, 0)),
                  pl.BlockSpec((1, 1, qk_dim), lambda b, j, pt: (b, 0, 0))],
        out_specs=pl.BlockSpec((1, n_heads, c_dim), lambda b, j, pt: (b, 0, 0)),
        scratch_shapes=[pltpu.VMEM((2, pages, page_size, c_dim), cache_ckv.dtype),
                        pltpu.VMEM((2, pages, r_dim, page_size), cache_kpe_t.dtype),
                        pltpu.SemaphoreType.DMA((2, 2)),
                        pltpu.VMEM((streams, per * page_size, c_dim), BF16),
                        pltpu.VMEM((streams, r_dim, per * page_size), BF16),
                        pltpu.VMEM((streams, n_heads, 1), F32),
                        pltpu.VMEM((streams, n_heads, 1), F32),
                        pltpu.VMEM((streams, n_heads, c_dim), F32)],
    )
    kern = functools.partial(_decode_kernel, pages=pages, streams=streams, page_size=page_size,
                             c_dim=c_dim, r_dim=r_dim)
    return pl.pallas_call(
        kern,
        grid_spec=grid_spec,
        out_shape=jax.ShapeDtypeStruct((db, n_heads, c_dim), F32),
        compiler_params=_params("arbitrary", "arbitrary"),
        name="decode_attention",
    )(page_table, cache_ckv, cache_kpe_t, q_cat, new_cat)


def _qlat_kernel(q_ref, w_ref, o_ref):
    o_ref[0] = jnp.dot(q_ref[...], w_ref[0], preferred_element_type=F32)


def _ouv_kernel(x_ref, w_ref, o_any, o_ref):
    del o_any
    o_ref[...] = jnp.dot(x_ref[0], w_ref[...], preferred_element_type=F32).astype(o_ref.dtype)


def _pool_tok_kernel(u_ref, um_ref, w_ref, sc_ref, o_ref, z_ref, t_ref, *, n_meta, n_prompts):
    g = pl.program_id(1)
    is_prompt = pl.program_id(0) < n_prompts
    _zero_fill_extra_rows(o_ref, pl.program_id(0) == n_prompts)
    s_len = u_ref.shape[0]
    zp = POOL_ZERO_ROWS
    end = zp + n_meta + s_len
    z_ref[:zp, :] = jnp.zeros((zp, z_ref.shape[1]), F32)
    t_ref[:zp, :] = jnp.zeros((zp, t_ref.shape[1]), F32)
    z_ref[zp:zp + n_meta, :] = um_ref[...]
    z_ref[zp + n_meta:, :] = u_ref[...]

    def window_sum(win):
        src, dst = z_ref, t_ref
        d = 1
        while d < win:
            dst[zp:end, :] = src[zp:end, :] + src[zp - d:end - d, :]
            src, dst = dst, src
            d *= 2
        return src[zp + n_meta:end, :]

    for gi, win in enumerate(POOL_WINDOWS):
        @pl.when((g == gi) & is_prompt)
        def _(win=win):
            mix = window_sum(win) / float(win) - u_ref[...]
            y = jnp.dot(mix.astype(BF16), w_ref[0], preferred_element_type=F32)
            o_ref[...] = (y * sc_ref[...]).astype(o_ref.dtype)


def _pool_extra_kernel(st_ref, us_ref, um_ref, w_ref, sc_ref, o_any, o_ref, acc_ref, zz_ref,
                       *, n_state, pg, n_meta_pad):
    del o_any
    j = pl.program_id(0)

    @pl.when(j == 0)
    def _():
        acc_ref[...] = jnp.zeros_like(acc_ref)

    for gi, win in enumerate(POOL_WINDOWS):
        cols = slice(gi * pg, (gi + 1) * pg)

        @pl.when(n_state - j <= win - 1)
        def _(cols=cols):
            acc_ref[:, cols] += st_ref[0, :, cols]

    @pl.when(j == n_state - 1)
    def _():
        zpad = zz_ref.shape[0] - n_meta_pad
        zz_ref[:zpad, :] = jnp.zeros((zpad, zz_ref.shape[1]), F32)
        zz_ref[zpad:, :] = um_ref[...]
        t = lax.broadcasted_iota(jnp.int32, (n_meta_pad, 1), 0)
        for gi, win in enumerate(POOL_WINDOWS):
            cols = slice(gi * pg, (gi + 1) * pg)
            us = us_ref[:, cols]
            mix_s = (acc_ref[:, cols] + us) / float(win) - us
            wsum = zz_ref[zpad:, cols]
            for kk in range(1, win):
                wsum = wsum + zz_ref[zpad - kk:zpad - kk + n_meta_pad, cols]
            cnt = jnp.minimum(t + 1, win).astype(F32)
            mix_m = wsum / cnt - um_ref[:, cols]
            w = w_ref[gi]
            sc = sc_ref[:, cols]
            o_ref[:n_meta_pad, cols] = (jnp.dot(mix_m.astype(BF16), w, preferred_element_type=F32)
                                        * sc).astype(o_ref.dtype)
            o_ref[n_meta_pad:, cols] = (jnp.dot(mix_s.astype(BF16), w, preferred_element_type=F32)
                                        * sc).astype(o_ref.dtype)


def _conv3(x0, x1, x2, cwb):
    return cwb[3:4, :] + x0 * cwb[0:1, :] + x1 * cwb[1:2, :] + x2 * cwb[2:3, :]


def _ffn_up_kernel(x_ref, xm_ref, wg_ref, wv_ref, cwb_ref, st_ref, act_ref, tail_ref, samp_ref,
                   xs_ref, upg_a, upv_a, upg_b, upv_b, eg_ref, ev_ref, prevg_ref, prevv_ref,
                   *, bm, n_cols, n_meta, samp_lo, n_samp, tails, starts):
    i, j = pl.program_id(0), pl.program_id(1)
    last_tile = pl.num_programs(0) - 1
    bn = act_ref.shape[1]
    g_cols, v_cols = slice(0, bn), slice(bn, 2 * bn)

    @pl.when(j == 0)
    def _():
        xs_ref[:bm, :] = x_ref[...]
        xs_ref[bm:, :] = xm_ref[...]

    @pl.when((i == 0) & (j == 0))
    def _():
        prevg_ref[...] = jnp.zeros(prevg_ref.shape, F32)
        prevv_ref[...] = jnp.zeros(prevv_ref.shape, F32)

    k_chunk = xs_ref.shape[1] // N_K_CHUNKS
    groups = bm // 16
    piece_rows = [(groups // N_ROW_PIECES + (1 if p < groups % N_ROW_PIECES else 0)) * 16
                  for p in range(N_ROW_PIECES)]
    piece_start = [sum(piece_rows[:p]) for p in range(N_ROW_PIECES)]

    def product_chunk(c, acc):
        ks = slice(c * k_chunk, (c + 1) * k_chunk)
        xk = xs_ref[:, ks]
        pg = jnp.dot(xk, wg_ref[ks, :].astype(BF16), preferred_element_type=F32)
        pv = jnp.dot(xk, wv_ref[ks, :].astype(BF16), preferred_element_type=F32)
        return (pg, pv) if acc is None else (acc[0] + pg, acc[1] + pv)

    def finish_rows(p, up_g, up_v):
        jj = j - 1
        r0, n = piece_start[p], piece_rows[p]
        r1 = r0 + n

        def conv(up_ref, e_ref, prev_ref, cols):
            if r0 == 0:
                e_ref[0:8, :] = prev_ref[jj]
            e_ref[8 + r0:8 + r1, :] = up_ref[r0:r1, :]
            if r1 == bm:
                prev_ref[jj] = up_ref[bm - 8:bm, :]
            return _conv3(e_ref[6 + r0:6 + r1, :], e_ref[7 + r0:7 + r1, :], up_ref[r0:r1, :],
                          cwb_ref[0, :, cols])

        gate = conv(up_g, eg_ref, prevg_ref, g_cols)
        val = conv(up_v, ev_ref, prevv_ref, v_cols)
        act_ref[r0:r1, :] = (jax.nn.silu(gate) * val).astype(act_ref.dtype)

    def redo_sequence_start(lo, after_meta, up_g, up_v):
        def conv(up_ref, e_ref, cols):
            if after_meta:
                e_ref[0:8, :] = up_ref[bm + n_meta - 8:bm + n_meta, :]
            else:
                e_ref[0:8, :] = jnp.zeros((8, e_ref.shape[1]), F32)
            e_ref[8:16, :] = up_ref[lo:lo + 8, :]
            return _conv3(e_ref[6:14, :], e_ref[7:15, :], up_ref[lo:lo + 8, :], cwb_ref[0, :, cols])

        gate = conv(up_g, eg_ref, g_cols)
        val = conv(up_v, ev_ref, v_cols)
        act_ref[lo:lo + 8, :] = (jax.nn.silu(gate) * val).astype(act_ref.dtype)

    def run(write, read):
        acc = None
        per = N_K_CHUNKS // N_ROW_PIECES
        for c in range(N_K_CHUNKS):
            if write is not None:
                acc = product_chunk(c, acc)
            if read is not None and c % per == 0:
                finish_rows(c // per, *read)
        if write is not None:
            write[0][...] = acc[0]
            write[1][...] = acc[1]
        if read is not None:
            finish_special_rows(*read)

    def finish_special_rows(up_g, up_v):
        for tile, lo, after_meta in starts:
            @pl.when(i == tile)
            def _(lo=lo, after_meta=after_meta):
                redo_sequence_start(lo, after_meta, up_g, up_v)

        samp = slice(samp_lo, samp_lo + n_samp)

        @pl.when(i == last_tile)
        def _():
            gate_s = _conv3(st_ref[0, :, 0:bn], st_ref[0, :, 2 * bn:3 * bn], up_g[samp, :],
                            cwb_ref[0, :, g_cols])
            val_s = _conv3(st_ref[0, :, bn:2 * bn], st_ref[0, :, 3 * bn:4 * bn], up_v[samp, :],
                           cwb_ref[0, :, v_cols])
            act_ref[samp, :] = (jax.nn.silu(gate_s) * val_s).astype(act_ref.dtype)
            samp_ref[0, :, g_cols] = up_g[samp, :]
            samp_ref[0, :, v_cols] = up_v[samp, :]

        @pl.when(i != last_tile)
        def _():
            samp_ref[...] = jnp.zeros(samp_ref.shape, F32)

        has_tail = i == tails[0][0]
        for tile, lo in tails:
            has_tail = has_tail | (i == tile)

            @pl.when(i == tile)
            def _(lo=lo):
                tail_ref[0, 0, :, g_cols] = up_g[lo:lo + 8, :]
                tail_ref[0, 0, :, v_cols] = up_v[lo:lo + 8, :]

        @pl.when(jnp.logical_not(has_tail))
        def _():
            tail_ref[...] = jnp.zeros(tail_ref.shape, F32)

    buffers = ((upg_a, upv_a), (upg_b, upv_b))
    for parity in (0, 1):
        write, read = buffers[parity], buffers[1 - parity]
        is_mine = j % 2 == parity
        if parity == 0:
            @pl.when(j == 0)
            def _(write=write):
                run(write, None)

        @pl.when(is_mine & (j > 0) & (j < n_cols))
        def _(write=write, read=read):
            run(write, read)

        if n_cols % 2 == parity:
            @pl.when(j == n_cols)
            def _(read=read):
                run(None, read)


def kernel(x_prompt, x_sample, cache_ckv, cache_kpe, state_pool, state_conv, page_table, meta_tokens,
           norm_mix, w_in, g_qnorm, w_uq, g_kvnorm, w_uk, w_uv, w_attn_branch, w_pool_grp, pool_scale,
           w_pool_branch, w_out, norm_ffn, w_up, conv_w, conv_b, w_down, norm_final):
    nb, s_len, d = x_prompt.shape
    db, dec_t, _ = x_sample.shape
    depth = w_in.shape[0]
    assert depth == 1 and dec_t == 1
    nm = meta_tokens.shape[0]
    _, c_dim, n_heads, d_nope = w_uk.shape
    v_dim = w_uv.shape[3]
    r_dim = cache_kpe.shape[3]
    half = r_dim // 2
    q_lora = g_qnorm.shape[1]
    pw = pool_scale.shape[1]
    n_grp = len(POOL_WINDOWS)
    pg = pw // n_grp
    n_state = state_pool.shape[2]
    conv_taps = conv_w.shape[1]
    dff2 = w_up.shape[2]
    dff = dff2 // 2
    page_size = cache_ckv.shape[2]
    n_pages = page_table.shape[1]
    past = n_pages * page_size
    scale = float(d_nope + r_dim) ** -0.5 * LOG2_E
    hd = 2 * LANES
    assert d_nope == LANES and v_dim == LANES and r_dim <= LANES and conv_taps == 3
    assert nm >= max(POOL_WINDOWS) - 1 and nm % 16 == 0 and nm <= META_PAD
    assert n_state == max(POOL_WINDOWS) - 1

    rt = nb * s_len
    r_meta = rt
    r_samp = rt + META_PAD
    e_rows = META_PAD + db
    m_rows = rt + e_rows
    assert rt % META_PAD == 0 and r_samp % db == 0 and db % 16 == 0 and r_meta % nm == 0

    off_q, off_kv, off_kr, off_gp = pw, pw + q_lora, pw + q_lora + c_dim, pw + q_lora + c_dim + r_dim
    assert off_q % q_lora == 0 and off_kv % c_dim == 0 and off_kr % LANES == 0 and pw % pg == 0

    n_a = off_kr + LANES
    n_a_pad = -(-n_a // 256) * 256
    w_a = jnp.concatenate([w_in[0, :, :off_gp], jnp.zeros((d, n_a_pad - off_gp), F32)], axis=1).astype(BF16)
    w_in_t = jnp.swapaxes(w_in[0], 0, 1)
    wq = w_uq[0].reshape(q_lora, n_heads, d_nope + r_dim)
    wq = jnp.concatenate([wq, jnp.zeros((q_lora, n_heads, hd - d_nope - r_dim), F32)], axis=2)
    wq = wq.reshape(q_lora, n_heads * hd).astype(BF16)
    w_uk2 = w_uk[0].reshape(c_dim, n_heads * d_nope).astype(BF16)
    w_ukt = jnp.transpose(w_uk[0], (1, 2, 0)).astype(BF16)
    w_uv2 = w_uv[0].reshape(c_dim, n_heads * v_dim).astype(BF16)
    w_pg = w_pool_grp[0].astype(BF16)
    w_d = w_down[0].astype(BF16)
    w_ab, w_pb, w_o, w_u = w_attn_branch[0], w_pool_branch[0], w_out[0], w_up[0]

    pos = jnp.concatenate([
        jnp.tile(nm + jnp.arange(s_len), nb),
        jnp.arange(nm), jnp.zeros((META_PAD - nm,), jnp.int32),
        jnp.full((db,), past, jnp.int32)])
    inv = ROPE_BASE ** (-jnp.arange(0, r_dim, 2, dtype=F32) / r_dim)
    ang = pos.astype(F32)[:, None] * inv[None, :]
    cos, sin = jnp.cos(ang), jnp.sin(ang)
    zl = jnp.zeros((m_rows, LANES - 2 * half), F32)
    zh = jnp.zeros((m_rows, half), F32)
    tab_c = jnp.concatenate([cos, cos, zl], axis=1)
    tab_s1 = jnp.concatenate([-sin, zh, zl], axis=1)
    tab_s2 = jnp.concatenate([zh, sin, zl], axis=1)

    bm = _pick_block(m_rows, 1100, 16)
    row_tab_spec = pl.BlockSpec((bm, LANES), lambda i, j: (i, 0))

    x_extra = jnp.concatenate([meta_tokens.astype(F32), jnp.zeros((META_PAD - nm, d), F32),
                               x_sample.reshape(db, d)], axis=0)
    rb = _pick_block(e_rows, 256, 16)
    assert rt % rb == 0
    x_all, h = _norm_in(x_prompt.reshape(rt, d), x_extra, norm_mix[0], rb)

    bn_a = _pick_block(n_a_pad, 768, LANES)
    u1 = _matmul(h, w_a, m=m_rows, k=d, n=n_a_pad, bm=bm, bn=bn_a, out_cols=n_a_pad, out_bn=bn_a,
                 out_dtype=F32, name="in_proj")

    def gate_epilogue(acc, o_ref):
        o_ref[...] = jax.nn.sigmoid(acc).astype(o_ref.dtype)

    bn_d = _pick_block(d, 512, LANES)
    gates = _matmul(h, w_in_t, m=m_rows, k=d, n=2 * d, bm=bm, bn=bn_d, out_cols=2 * d, out_bn=bn_d,
                    out_dtype=BF16, name="gate_proj", epilogue=gate_epilogue, w_nk_row0=off_gp)

    def q_prologue(x_ref, g_ref):
        return _rms(x_ref[...], g_ref[...])

    bn_q = _pick_block(n_heads * hd, 1024, hd)

    def q_epilogue(acc, o_ref, ct_ref, s1_ref, s2_ref):
        acc = acc * scale
        for cb in range(bn_q // LANES):
            cols = slice(cb * LANES, (cb + 1) * LANES)
            v = acc[:, cols]
            if cb % 2 == 1:
                v = _rope_tile(v, ct_ref[...], s1_ref[...], s2_ref[...], half)
            o_ref[:, cols] = v.astype(o_ref.dtype)

    q_all = _matmul(u1, wq, m=m_rows, k=q_lora, n=n_heads * hd, bm=bm, bn=bn_q,
                    out_cols=n_heads * hd, out_bn=bn_q, out_dtype=BF16, name="q_proj",
                    x_col_block=off_q // q_lora,
                    pro=(g_qnorm[0].reshape(1, q_lora),), pro_specs=(pl.BlockSpec((1, q_lora), lambda i, j: (0, 0)),),
                    prologue=q_prologue,
                    extra=(tab_c, tab_s1, tab_s2), extra_specs=(row_tab_spec,) * 3, epilogue=q_epilogue)

    tab1 = pl.BlockSpec((bm, LANES), lambda i: (i, 0))
    ckv32, ckv16, kpe32 = pl.pallas_call(
        functools.partial(_kv_prep_kernel, half=half),
        grid=(m_rows // bm,),
        in_specs=[pl.BlockSpec((bm, c_dim), lambda i: (i, off_kv // c_dim)),
                  pl.BlockSpec((bm, LANES), lambda i: (i, off_kr // LANES)),
                  pl.BlockSpec((1, c_dim), lambda i: (0, 0)),
                  tab1, tab1, tab1],
        out_specs=[pl.BlockSpec((bm, c_dim), lambda i: (i, 0)),
                   pl.BlockSpec((bm, c_dim), lambda i: (i, 0)),
                   pl.BlockSpec((bm, LANES), lambda i: (i, 0))],
        out_shape=[jax.ShapeDtypeStruct((m_rows, c_dim), F32),
                   jax.ShapeDtypeStruct((m_rows, c_dim), BF16),
                   jax.ShapeDtypeStruct((m_rows, LANES), F32)],
        compiler_params=_params("arbitrary"),
        name="kv_prep",
    )(u1, u1, g_kvnorm[0].reshape(1, c_dim), tab_c, tab_s1, tab_s2)

    hps = _pick_block(n_heads, 16, 1)

    def k_epilogue(acc, o_ref, kpe_ref):
        kr = kpe_ref[...].astype(o_ref.dtype)
        for hh in range(hps):
            o_ref[:, hh * hd:hh * hd + d_nope] = acc[:, hh * d_nope:(hh + 1) * d_nope].astype(o_ref.dtype)
            o_ref[:, hh * hd + d_nope:(hh + 1) * hd] = kr

    k_all = _matmul(ckv16, w_uk2, m=m_rows, k=c_dim, n=n_heads * d_nope, bm=bm, bn=hps * d_nope,
                    out_cols=n_heads * hd, out_bn=hps * hd, out_dtype=BF16, name="k_up",
                    extra=(kpe32,), extra_specs=(row_tab_spec,), epilogue=k_epilogue)
    bn_v = _pick_block(n_heads * v_dim, 2048, LANES)
    v_all = _matmul(ckv16, w_uv2, m=m_rows, k=c_dim, n=n_heads * v_dim, bm=bm, bn=bn_v,
                    out_cols=n_heads * v_dim, out_bn=bn_v, out_dtype=BF16, name="v_up")

    bq = _pick_block(s_len, 512, 128)
    mb = r_meta // META_PAD
    assert e_rows <= s_len

    def prompt_block(b, col, col_off=0):
        return jnp.minimum(b, nb - 1), jnp.where(b < nb, col, 0) + col_off
    o_all = pl.pallas_call(
        functools.partial(_attn_tok_kernel, bq=bq, n_meta=nm, n_prompts=nb),
        grid=(nb + 1, n_heads),
        in_specs=[pl.BlockSpec((s_len, hd), prompt_block),
                  pl.BlockSpec((s_len, hd), prompt_block),
                  pl.BlockSpec((s_len, v_dim), prompt_block),
                  pl.BlockSpec((META_PAD, hd), lambda b, hh: (mb, hh)),
                  pl.BlockSpec((META_PAD, v_dim), lambda b, hh: (mb, hh))],
        out_specs=pl.BlockSpec((s_len, v_dim), lambda b, hh: (b, hh)),
        out_shape=jax.ShapeDtypeStruct((m_rows, n_heads * v_dim), BF16),
        compiler_params=_params("parallel", "arbitrary"),
        name="attn_prompt",
    )(q_all, k_all, v_all, k_all, v_all)

    o_all = pl.pallas_call(
        functools.partial(_attn_meta_kernel, n_meta=nm),
        grid=(n_heads,),
        in_specs=[pl.BlockSpec((META_PAD, hd), lambda hh: (mb, hh)),
                  pl.BlockSpec((META_PAD, hd), lambda hh: (mb, hh)),
                  pl.BlockSpec((META_PAD, v_dim), lambda hh: (mb, hh)),
                  pl.BlockSpec(memory_space=pl.ANY)],
        out_specs=pl.BlockSpec((META_PAD, v_dim), lambda hh: (mb, hh)),
        out_shape=jax.ShapeDtypeStruct((m_rows, n_heads * v_dim), BF16),
        input_output_aliases={3: 0},
        compiler_params=_params("arbitrary"),
        name="attn_meta",
    )(q_all, k_all, v_all, o_all)

    sb = r_samp // db
    q_lat = pl.pallas_call(
        _qlat_kernel,
        grid=(n_heads,),
        in_specs=[pl.BlockSpec((db, d_nope), lambda hh: (sb, hh * (hd // d_nope))),
                  pl.BlockSpec((1, d_nope, c_dim), lambda hh: (hh, 0, 0))],
        out_specs=pl.BlockSpec((1, db, c_dim), lambda hh: (hh, 0, 0)),
        out_shape=jax.ShapeDtypeStruct((n_heads, db, c_dim), F32),
        compiler_params=_params("arbitrary"),
        name="q_absorb",
    )(q_all, w_ukt)
    qk_dim = -(-(c_dim + r_dim) // LANES) * LANES
    q_samp = q_all[r_samp:r_samp + db].reshape(db, n_heads, hd)
    q_cat = jnp.concatenate([jnp.transpose(q_lat, (1, 0, 2)).astype(BF16),
                             q_samp[:, :, d_nope:d_nope + r_dim],
                             jnp.zeros((db, n_heads, qk_dim - c_dim - r_dim), BF16)], axis=2)
    new_cat = jnp.concatenate([ckv32[r_samp:r_samp + db], kpe32[r_samp:r_samp + db, :r_dim],
                               jnp.zeros((db, qk_dim - c_dim - r_dim), F32)], axis=1).reshape(db, 1, qk_dim)
    pages = _pick_block(n_pages, 32, 1)
    streams = _pick_block(pages, 4, 1)
    cache_kpe_t = jnp.swapaxes(cache_kpe[0], 1, 2)
    o_lat = _decode_attention(page_table, cache_ckv[0], cache_kpe_t, q_cat, new_cat, pages=pages, streams=streams)
    o_lat_t = jnp.transpose(o_lat, (1, 0, 2)).astype(BF16)
    o_all = pl.pallas_call(
        _ouv_kernel,
        grid=(n_heads,),
        in_specs=[pl.BlockSpec((1, db, c_dim), lambda hh: (hh, 0, 0)),
                  pl.BlockSpec((c_dim, v_dim), lambda hh: (0, hh)),
                  pl.BlockSpec(memory_space=pl.ANY)],
        out_specs=pl.BlockSpec((db, v_dim), lambda hh: (sb, hh)),
        out_shape=jax.ShapeDtypeStruct((m_rows, n_heads * v_dim), BF16),
        input_output_aliases={2: 0},
        compiler_params=_params("arbitrary"),
        name="o_up_decode",
    )(o_lat_t, w_uv2, o_all)

    scale_row = pool_scale[0].reshape(1, pw)
    pm = pl.pallas_call(
        functools.partial(_pool_tok_kernel, n_meta=nm, n_prompts=nb),
        grid=(nb + 1, n_grp),
        in_specs=[pl.BlockSpec((s_len, pg), prompt_block),
                  pl.BlockSpec((nm, pg), lambda b, g: (r_meta // nm, g)),
                  pl.BlockSpec((1, pg, pg), lambda b, g: (g, 0, 0)),
                  pl.BlockSpec((1, pg), lambda b, g: (0, g))],
        out_specs=pl.BlockSpec((s_len, pg), lambda b, g: (b, g)),
        out_shape=jax.ShapeDtypeStruct((m_rows, pw), BF16),
        scratch_shapes=[pltpu.VMEM((POOL_ZERO_ROWS + nm + s_len, pg), F32),
                        pltpu.VMEM((POOL_ZERO_ROWS + nm + s_len, pg), F32)],
        compiler_params=_params("parallel", "arbitrary"),
        name="pool_prompt",
    )(u1, u1, w_pg, scale_row)
    st_rows = jnp.transpose(state_pool[0], (1, 0, 2))
    zrows = POOL_ZERO_ROWS
    pm = pl.pallas_call(
        functools.partial(_pool_extra_kernel, n_state=n_state, pg=pg, n_meta_pad=META_PAD),
        grid=(n_state,),
        in_specs=[pl.BlockSpec((1, db, pw), lambda j: (j, 0, 0)),
                  pl.BlockSpec((db, pw), lambda j: (sb, 0)),
                  pl.BlockSpec((META_PAD, pw), lambda j: (mb, 0)),
                  pl.BlockSpec((n_grp, pg, pg), lambda j: (0, 0, 0)),
                  pl.BlockSpec((1, pw), lambda j: (0, 0)),
                  pl.BlockSpec(memory_space=pl.ANY)],
        out_specs=pl.BlockSpec((e_rows, pw), lambda j: (rt // e_rows, 0)),
        out_shape=jax.ShapeDtypeStruct((m_rows, pw), BF16),
        scratch_shapes=[pltpu.VMEM((db, pw), F32), pltpu.VMEM((zrows + META_PAD, pw), F32)],
        input_output_aliases={5: 0},
        compiler_params=_params("arbitrary"),
        name="pool_extra",
    )(st_rows, u1, u1, w_pg, scale_row, pm)

    def pool_branch_epilogue(acc, o_ref, g_ref):
        o_ref[...] = (g_ref[...].astype(F32) * acc).astype(o_ref.dtype)

    tile_spec = pl.BlockSpec((bm, bn_d), lambda i, j: (i, j))
    bn_p = _pick_block(d, 1024, LANES)
    t1 = _matmul(pm, w_pb, m=m_rows, k=pw, n=d, bm=bm, bn=bn_p, out_cols=d, out_bn=bn_p, out_dtype=BF16,
                 name="pool_branch", extra=(gates,),
                 extra_specs=(pl.BlockSpec((bm, bn_p), lambda i, j: (i, j)),), epilogue=pool_branch_epilogue)

    def attn_branch_epilogue(acc, o_ref, g_ref, t_ref):
        o_ref[...] = (t_ref[...].astype(F32) + g_ref[...].astype(F32) * acc).astype(o_ref.dtype)

    mixed = _matmul(o_all, w_ab, m=m_rows, k=n_heads * v_dim, n=d, bm=bm, bn=bn_d, out_cols=d, out_bn=bn_d,
                    out_dtype=BF16, name="attn_branch", extra=(gates, t1),
                    extra_specs=(pl.BlockSpec((bm, bn_d), lambda i, j: (i, j + d // bn_d)), tile_spec),
                    epilogue=attn_branch_epilogue)

    def residual_epilogue(acc, o_ref, r_ref):
        o_ref[...] = r_ref[...] + acc

    x1 = _matmul(mixed, w_o, m=m_rows, k=d, n=d, bm=bm, bn=bn_d, out_cols=d, out_bn=bn_d, out_dtype=F32,
                 name="out_proj", extra=(x_all,), extra_specs=(tile_spec,), epilogue=residual_epilogue)

    rb2 = _pick_block(m_rows, 512, 16)
    h2 = _norm_rows(x1, norm_ffn[0], row0=0, nrows=m_rows, rb=rb2, out_dtype=BF16, name="norm_ffn")
    bc = _pick_block(dff, 256, LANES)
    ncb = dff // bc
    n_mt = m_rows // bm
    last_mt = n_mt - 1
    samp_lo = r_samp - last_mt * bm
    assert samp_lo >= 0 and r_samp + db == m_rows
    tails = tuple(divmod((b + 1) * s_len - 8, bm) for b in range(nb))
    assert len({t for t, _ in tails}) == nb and all(lo + 8 <= bm for _, lo in tails)
    starts = tuple((*divmod(b * s_len, bm), True) for b in range(nb)) + ((*divmod(rt, bm), False),)
    assert nm >= 8 and all(lo % 8 == 0 for _, lo, _ in starts)

    def by_block(a):
        rows = a.shape[0]
        g = a[:, :dff].reshape(rows, ncb, bc)
        v = a[:, dff:].reshape(rows, ncb, bc)
        return jnp.transpose(jnp.concatenate([g, v], axis=2), (1, 0, 2))

    def from_block(a):
        g = jnp.swapaxes(a[..., :bc], -3, -2)
        v = jnp.swapaxes(a[..., bc:], -3, -2)
        lead = g.shape[:-2]
        return jnp.concatenate([g.reshape(*lead, dff), v.reshape(*lead, dff)], axis=-1)

    cwb = by_block(jnp.concatenate([conv_w[0], conv_b[0][None], jnp.zeros((8 - conv_taps - 1, dff2), F32)], axis=0))
    st_blk = jnp.concatenate([by_block(state_conv[0, :, 0]), by_block(state_conv[0, :, 1])], axis=2)

    def done_block(j):
        return jnp.maximum(j - 1, 0)

    act, tail_blk, samp_blk = pl.pallas_call(
        functools.partial(_ffn_up_kernel, bm=bm, n_cols=ncb, n_meta=nm,
                          samp_lo=samp_lo, n_samp=db, tails=tails, starts=starts),
        grid=(n_mt, ncb + 1),
        in_specs=[pl.BlockSpec((bm, d), lambda i, j: (i, 0), pipeline_mode=pl.Buffered(1)),
                  pl.BlockSpec((nm, d), lambda i, j: (r_meta // nm, 0)),
                  pl.BlockSpec((d, bc), lambda i, j: (0, jnp.minimum(j, ncb - 1))),
                  pl.BlockSpec((d, bc), lambda i, j: (0, jnp.minimum(j, ncb - 1) + ncb)),
                  pl.BlockSpec((1, 8, 2 * bc), lambda i, j: (done_block(j), 0, 0)),
                  pl.BlockSpec((1, db, 4 * bc), lambda i, j: (jnp.where(i == last_mt, done_block(j), 0), 0, 0))],
        out_specs=[pl.BlockSpec((bm, bc), lambda i, j: (i, done_block(j))),
                   pl.BlockSpec((1, 1, 8, 2 * bc), lambda i, j: (i, done_block(j), 0, 0)),
                   pl.BlockSpec((1, db, 2 * bc),
                                lambda i, j: (jnp.where(i == last_mt, last_mt + done_block(j), i), 0, 0))],
        out_shape=[jax.ShapeDtypeStruct((m_rows, dff), BF16),
                   jax.ShapeDtypeStruct((n_mt, ncb, 8, 2 * bc), F32),
                   jax.ShapeDtypeStruct((last_mt + ncb, db, 2 * bc), F32)],
        scratch_shapes=[pltpu.VMEM((bm + nm, d), BF16),
                        pltpu.VMEM((bm + nm, bc), F32), pltpu.VMEM((bm + nm, bc), F32),
                        pltpu.VMEM((bm + nm, bc), F32), pltpu.VMEM((bm + nm, bc), F32),
                        pltpu.VMEM((8 + bm, bc), F32), pltpu.VMEM((8 + bm, bc), F32),
                        pltpu.VMEM((ncb, 8, bc), F32), pltpu.VMEM((ncb, 8, bc), F32)],
        compiler_params=_params("arbitrary", "arbitrary"),
        name="ffn_up",
    )(h2, h2, w_u, w_u, cwb, st_blk)
    n_tail = conv_taps - 1
    tail_tiles = jnp.array([t for t, _ in tails])
    up_tail = from_block(tail_blk[tail_tiles])[:, 8 - n_tail:]
    up_samp = from_block(samp_blk[last_mt:])

    bm_d = _pick_block(m_rows, 528, 16)
    bn_dn = _pick_block(d, 512, LANES)
    x2 = _matmul(act, w_d, m=m_rows, k=dff, n=d, bm=bm_d, bn=bn_dn, out_cols=d, out_bn=bn_dn, out_dtype=F32,
                 name="ffn_down", extra=(x1,), extra_specs=(pl.BlockSpec((bm_d, bn_dn), lambda i, j: (i, j)),),
                 epilogue=residual_epilogue)

    rb3 = _pick_block(rt, 512, 16)
    y_prompt = _norm_rows(x2, norm_final, row0=0, nrows=rt, rb=rb3, out_dtype=F32, name="norm_final_prompt")
    y_sample = _norm_rows(x2, norm_final, row0=r_samp, nrows=db, rb=db, out_dtype=F32, name="norm_final_sample")

    def with_meta(a, width):
        tok = a[:rt, :width].reshape(nb, s_len, width)
        meta = jnp.broadcast_to(a[r_meta:r_meta + nm, :width][None], (nb, nm, width))
        return jnp.concatenate([meta, tok], axis=1)[None]

    def prompt_tail(a, n_tail, width):
        return jnp.stack([a[(b + 1) * s_len - n_tail:(b + 1) * s_len, :width] for b in range(nb)])[None]

    return (
        y_prompt.reshape(nb, s_len, d),
        y_sample.reshape(db, 1, d),
        with_meta(ckv32, c_dim),
        with_meta(kpe32, r_dim),
        prompt_tail(u1, n_state, pw),
        up_tail[None],
        ckv32[r_samp:r_samp + db].reshape(1, db, 1, c_dim),
        kpe32[r_samp:r_samp + db, :r_dim].reshape(1, db, 1, r_dim),
        jnp.concatenate([state_pool[0, :, 1:], u1[r_samp:r_samp + db, :pw][:, None]], axis=1)[None],
        jnp.concatenate([state_conv[0, :, 1:], up_samp[:, None]], axis=1)[None],
    )
```

```python
import functools

import jax
import jax.numpy as jnp
from jax import lax
from jax.experimental import pallas as pl
from jax.experimental.pallas import tpu as pltpu

EPS = 1e-6
ROPE_BASE = 10000.0
POOL_WINDOWS = (2, 4, 8, 16)
LANES = 128
META_PAD = 128
POOL_ZERO_ROWS = 16
MASK_VALUE = -1e30
LOG2_E = 1.4426950408889634
N_K_CHUNKS = 16
N_ROW_PIECES = 8
VMEM_LIMIT_BYTES = 60 * 1024 * 1024

F32 = jnp.float32
BF16 = jnp.bfloat16


def _params(*sem):
    return pltpu.CompilerParams(dimension_semantics=sem, vmem_limit_bytes=VMEM_LIMIT_BYTES)


def _pick_block(n, cap, mult):
    best = None
    for d in range(mult, min(n, cap) + 1, mult):
        if n % d == 0:
            best = d
    assert best is not None, (n, cap, mult)
    return best


def _rms(x, g):
    return x * lax.rsqrt(jnp.mean(x * x, axis=-1, keepdims=True) + EPS) * g


def _norm_in_kernel(xt_ref, xe_ref, g_ref, x_ref, h_ref, *, n_tok_blocks):
    i = pl.program_id(0)

    def emit(src):
        x = src[...]
        x_ref[...] = x
        h_ref[...] = _rms(x, g_ref[...]).astype(h_ref.dtype)

    @pl.when(i < n_tok_blocks)
    def _():
        emit(xt_ref)

    @pl.when(i >= n_tok_blocks)
    def _():
        emit(xe_ref)


def _norm_in(x_tok, x_extra, g, rb):
    rt, d = x_tok.shape
    e = x_extra.shape[0]
    m = rt + e
    ntb, neb = rt // rb, e // rb
    return pl.pallas_call(
        functools.partial(_norm_in_kernel, n_tok_blocks=ntb),
        grid=(ntb + neb,),
        in_specs=[
            pl.BlockSpec((rb, d), lambda i: (jnp.minimum(i, ntb - 1), 0)),
            pl.BlockSpec((rb, d), lambda i: (jnp.maximum(i - ntb, 0), 0)),
            pl.BlockSpec((1, d), lambda i: (0, 0)),
        ],
        out_specs=[pl.BlockSpec((rb, d), lambda i: (i, 0)),
                   pl.BlockSpec((rb, d), lambda i: (i, 0))],
        out_shape=[jax.ShapeDtypeStruct((m, d), F32), jax.ShapeDtypeStruct((m, d), BF16)],
        compiler_params=_params("arbitrary"),
        name="norm_in",
    )(x_tok, x_extra, g.reshape(1, d))


def _norm_rows_kernel(x_ref, g_ref, o_ref):
    o_ref[...] = _rms(x_ref[...], g_ref[...]).astype(o_ref.dtype)


def _norm_rows(x, g, *, row0, nrows, rb, out_dtype, name):
    d = x.shape[1]
    assert row0 % rb == 0 and nrows % rb == 0
    b0 = row0 // rb
    return pl.pallas_call(
        _norm_rows_kernel,
        grid=(nrows // rb,),
        in_specs=[pl.BlockSpec((rb, d), lambda i: (b0 + i, 0)),
                  pl.BlockSpec((1, d), lambda i: (0, 0))],
        out_specs=pl.BlockSpec((rb, d), lambda i: (i, 0)),
        out_shape=jax.ShapeDtypeStruct((nrows, d), out_dtype),
        compiler_params=_params("arbitrary"),
        name=name,
    )(x, g.reshape(1, d))


def _mm_kernel(*refs, n_pro, n_extra, prologue, epilogue, w_is_nk=False):
    x_ref, w_ref = refs[0], refs[1]
    pro_refs = refs[2:2 + n_pro]
    extra_refs = refs[2 + n_pro:2 + n_pro + n_extra]
    o_ref = refs[2 + n_pro + n_extra]
    if prologue is not None:
        xs_ref = refs[3 + n_pro + n_extra]

        @pl.when(pl.program_id(1) == 0)
        def _():
            xs_ref[...] = prologue(x_ref, *pro_refs).astype(xs_ref.dtype)

        xv = xs_ref[...]
    else:
        xv = x_ref[...]
    contract = (((1,), (1,)), ((), ())) if w_is_nk else (((1,), (0,)), ((), ()))
    acc = lax.dot_general(xv, w_ref[...].astype(BF16), contract, preferred_element_type=F32)
    epilogue(acc, o_ref, *extra_refs)


def _store_epilogue(acc, o_ref):
    o_ref[...] = acc.astype(o_ref.dtype)


def _matmul(x, w, *, m, k, n, bm, bn, out_cols, out_bn, out_dtype, name,
            x_col_block=0, pro=(), pro_specs=(), prologue=None,
            extra=(), extra_specs=(), epilogue=_store_epilogue, w_nk_row0=None):
    assert m % bm == 0 and n % bn == 0
    scratch = [pltpu.VMEM((bm, k), BF16)] if prologue is not None else []
    kern = functools.partial(_mm_kernel, n_pro=len(pro), n_extra=len(extra),
                             prologue=prologue, epilogue=epilogue, w_is_nk=w_nk_row0 is not None)
    if w_nk_row0 is None:
        w_spec = pl.BlockSpec((k, bn), lambda i, j: (0, j))
    else:
        assert w_nk_row0 % 8 == 0
        w_spec = pl.BlockSpec((pl.Element(bn), pl.Element(k)),
                              lambda i, j: (pl.multiple_of(w_nk_row0 + j * bn, 8), 0))
    return pl.pallas_call(
        kern,
        grid=(m // bm, n // bn),
        in_specs=[pl.BlockSpec((bm, k), lambda i, j: (i, x_col_block)),
                  w_spec,
                  *pro_specs, *extra_specs],
        out_specs=pl.BlockSpec((bm, out_bn), lambda i, j: (i, j)),
        out_shape=jax.ShapeDtypeStruct((m, out_cols), out_dtype),
        scratch_shapes=scratch,
        compiler_params=_params("parallel", "arbitrary"),
        name=name,
    )(x, w, *pro, *extra)


def _rope_tile(v, ct, s1, s2, half):
    return v * ct + pltpu.roll(v, LANES - half, 1) * s1 + pltpu.roll(v, half, 1) * s2


def _kv_prep_kernel(ukv_ref, ukr_ref, g_ref, ct_ref, s1_ref, s2_ref, c32_ref, c16_ref, kpe_ref, *, half):
    c = _rms(ukv_ref[...], g_ref[...])
    c32_ref[...] = c
    c16_ref[...] = c.astype(BF16)
    kpe_ref[...] = _rope_tile(ukr_ref[...], ct_ref[...], s1_ref[...], s2_ref[...], half)


def _softmax_step(q, k, v, carry, mask=None):
    m_i, l_i, acc = carry
    s = lax.dot_general(q, k, (((1,), (1,)), ((), ())), preferred_element_type=F32)
    if mask is not None:
        s = jnp.where(mask, s, MASK_VALUE)
    m_new = jnp.maximum(m_i, jnp.max(s, axis=1, keepdims=True))
    alpha = jnp.exp2(m_i - m_new)
    p = jnp.exp2(s - m_new)
    l_new = alpha * l_i + jnp.sum(p, axis=1, keepdims=True)
    acc_new = alpha * acc + jnp.dot(p.astype(v.dtype), v, preferred_element_type=F32)
    return m_new, l_new, acc_new


def _zero_fill_extra_rows(o_ref, is_extra_step):
    @pl.when(is_extra_step)
    def _():
        o_ref[...] = jnp.zeros(o_ref.shape, o_ref.dtype)


def _attn_tok_kernel(q_ref, k_ref, v_ref, km_ref, vm_ref, o_ref, *, bq, n_meta, n_prompts):
    s_len, vdim = v_ref.shape
    _zero_fill_extra_rows(o_ref, pl.program_id(0) == n_prompts)

    @pl.when(pl.program_id(0) < n_prompts)
    def _():
        meta_mask = lax.broadcasted_iota(jnp.int32, (bq, km_ref.shape[0]), 1) < n_meta
        causal = (lax.broadcasted_iota(jnp.int32, (bq, bq), 1) <= lax.broadcasted_iota(jnp.int32, (bq, bq), 0))
        for qi in range(s_len // bq):
            rows = slice(qi * bq, (qi + 1) * bq)
            q = q_ref[rows, :]
            carry = (jnp.full((bq, 1), MASK_VALUE, F32), jnp.zeros((bq, 1), F32), jnp.zeros((bq, vdim), F32))
            carry = _softmax_step(q, km_ref[...], vm_ref[...], carry, meta_mask)
            for kj in range(qi):
                keys = slice(kj * bq, (kj + 1) * bq)
                carry = _softmax_step(q, k_ref[keys, :], v_ref[keys, :], carry)
            carry = _softmax_step(q, k_ref[rows, :], v_ref[rows, :], carry, causal)
            _, l_i, acc = carry
            o_ref[rows, :] = (acc / l_i).astype(o_ref.dtype)


def _attn_meta_kernel(q_ref, km_ref, vm_ref, o_any, o_ref, *, n_meta):
    del o_any
    n = q_ref.shape[0]
    row = lax.broadcasted_iota(jnp.int32, (n, n), 0)
    col = lax.broadcasted_iota(jnp.int32, (n, n), 1)
    mask = (col <= row) & (col < n_meta)
    carry = (jnp.full((n, 1), MASK_VALUE, F32), jnp.zeros((n, 1), F32),
             jnp.zeros((n, vm_ref.shape[1]), F32))
    _, l_i, acc = _softmax_step(q_ref[...], km_ref[...], vm_ref[...], carry, mask)
    o_ref[...] = (acc / l_i).astype(o_ref.dtype)


def _decode_kernel(pt_ref, ckv_hbm, kpe_hbm, q_ref, new_ref, o_ref,
                   ckv_buf, kpe_buf, sem, kc_ref, kp_ref, m_ref, l_ref, acc_ref,
                   *, pages, streams, page_size, c_dim, r_dim):
    b, j = pl.program_id(0), pl.program_id(1)
    n_b, n_j = pl.num_programs(0), pl.num_programs(1)
    step = b * n_j + j
    slot = step % 2
    per = pages // streams

    def page_copies(bb, jj, sl):
        copies = []
        for p in range(pages):
            page = pt_ref[bb, jj * pages + p]
            copies.append(pltpu.make_async_copy(ckv_hbm.at[page], ckv_buf.at[sl, p], sem.at[sl, 0]))
            copies.append(pltpu.make_async_copy(kpe_hbm.at[page], kpe_buf.at[sl, p], sem.at[sl, 1]))
        return copies

    @pl.when(step == 0)
    def _():
        for cp in page_copies(b, j, slot):
            cp.start()

    @pl.when(step + 1 < n_b * n_j)
    def _():
        wrap = j + 1 == n_j
        for cp in page_copies(jnp.where(wrap, b + 1, b), jnp.where(wrap, 0, j + 1), 1 - slot):
            cp.start()

    for cp in page_copies(b, j, slot):
        cp.wait()

    q = q_ref[0]
    q_lat, q_pe = q[:, :c_dim], q[:, c_dim:c_dim + r_dim]

    @pl.when(j == 0)
    def _():
        new = new_ref[0]
        s_self = jnp.sum(q.astype(F32) * new, axis=1, keepdims=True)
        m_ref[0] = s_self
        l_ref[0] = jnp.ones_like(s_self)
        acc_ref[0] = jnp.broadcast_to(new[:, :c_dim], acc_ref.shape[1:])
        for st in range(1, streams):
            m_ref[st] = jnp.full(m_ref.shape[1:], MASK_VALUE, F32)
            l_ref[st] = jnp.zeros(l_ref.shape[1:], F32)
            acc_ref[st] = jnp.zeros(acc_ref.shape[1:], F32)

    def scores(st):
        for p in range(per):
            keys = slice(p * page_size, (p + 1) * page_size)
            kc_ref[st, keys, :] = ckv_buf[slot, st * per + p].astype(kc_ref.dtype)
            kp_ref[st, :, keys] = kpe_buf[slot, st * per + p].astype(kp_ref.dtype)
        return (lax.dot_general(q_lat, kc_ref[st], (((1,), (1,)), ((), ())), preferred_element_type=F32)
                + jnp.dot(q_pe, kp_ref[st], preferred_element_type=F32))

    def softmax(st, s):
        m_old = m_ref[st]
        m_new = jnp.maximum(m_old, jnp.max(s, axis=1, keepdims=True))
        alpha = jnp.exp2(m_old - m_new)
        p_att = jnp.exp2(s - m_new)
        l_ref[st] = alpha * l_ref[st] + jnp.sum(p_att, axis=1, keepdims=True)
        m_ref[st] = m_new
        return alpha, p_att.astype(kc_ref.dtype)

    def values(st, alpha, p_att):
        acc_ref[st] = alpha * acc_ref[st] + jnp.dot(p_att, kc_ref[st], preferred_element_type=F32)

    pending_s, pending_p = None, None
    for st in range(streams):
        s = scores(st)
        if pending_p is not None:
            values(st - 2, *pending_p)
            pending_p = None
        if pending_s is not None:
            pending_p = softmax(st - 1, pending_s)
        pending_s = s
    if pending_p is not None:
        values(streams - 2, *pending_p)
    values(streams - 1, *softmax(streams - 1, pending_s))

    @pl.when(j == pl.num_programs(1) - 1)
    def _():
        m_all = m_ref[0]
        for st in range(1, streams):
            m_all = jnp.maximum(m_all, m_ref[st])
        l_all = jnp.zeros(l_ref.shape[1:], F32)
        acc_all = jnp.zeros(acc_ref.shape[1:], F32)
        for st in range(streams):
            w = jnp.exp2(m_ref[st] - m_all)
            l_all = l_all + w * l_ref[st]
            acc_all = acc_all + w * acc_ref[st]
        o_ref[0] = (acc_all / l_all).astype(o_ref.dtype)


def _decode_attention(page_table, cache_ckv, cache_kpe_t, q_cat, new_cat, *, pages, streams):
    db, n_pages = page_table.shape
    _, page_size, c_dim = cache_ckv.shape
    r_dim = cache_kpe_t.shape[1]
    n_heads, qk_dim = q_cat.shape[1], q_cat.shape[2]
    assert n_pages % pages == 0 and pages % streams == 0
    per = pages // streams

    grid_spec = pltpu.PrefetchScalarGridSpec(
        num_scalar_prefetch=1,
        grid=(db, n_pages // pages),
        in_specs=[pl.BlockSpec(memory_space=pl.ANY),
                  pl.BlockSpec(memory_space=pl.ANY),
                  pl.BlockSpec((1, n_heads, qk_dim), lambda b, j, pt: (b, 0, 0)),
                  pl.BlockSpec((1, 1, qk_dim), lambda b, j, pt: (b, 0, 0))],
        out_specs=pl.BlockSpec((1, n_heads, c_dim), lambda b, j, pt: (b, 0, 0)),
        scratch_shapes=[pltpu.VMEM((2, pages, page_size, c_dim), cache_ckv.dtype),
                        pltpu.VMEM((2, pages, r_dim, page_size), cache_kpe_t.dtype),
                        pltpu.SemaphoreType.DMA((2, 2)),
                        pltpu.VMEM((streams, per * page_size, c_dim), BF16),
                        pltpu.VMEM((streams, r_dim, per * page_size), BF16),
                        pltpu.VMEM((streams, n_heads, 1), F32),
                        pltpu.VMEM((streams, n_heads, 1), F32),
                        pltpu.VMEM((streams, n_heads, c_dim), F32)],
    )
    kern = functools.partial(_decode_kernel, pages=pages, streams=streams, page_size=page_size,
                             c_dim=c_dim, r_dim=r_dim)
    return pl.pallas_call(
        kern,
        grid_spec=grid_spec,
        out_shape=jax.ShapeDtypeStruct((db, n_heads, c_dim), F32),
        compiler_params=_params("arbitrary", "arbitrary"),
        name="decode_attention",
    )(page_table, cache_ckv, cache_kpe_t, q_cat, new_cat)


def _qlat_kernel(q_ref, w_ref, o_ref):
    o_ref[0] = jnp.dot(q_ref[...], w_ref[0], preferred_element_type=F32)


def _ouv_kernel(x_ref, w_ref, o_any, o_ref):
    del o_any
    o_ref[...] = jnp.dot(x_ref[0], w_ref[...], preferred_element_type=F32).astype(o_ref.dtype)


def _pool_tok_kernel(u_ref, um_ref, w_ref, sc_ref, o_ref, z_ref, t_ref, *, n_meta, n_prompts):
    g = pl.program_id(1)
    is_prompt = pl.program_id(0) < n_prompts
    _zero_fill_extra_rows(o_ref, pl.program_id(0) == n_prompts)
    s_len = u_ref.shape[0]
    zp = POOL_ZERO_ROWS
    end = zp + n_meta + s_len
    z_ref[:zp, :] = jnp.zeros((zp, z_ref.shape[1]), F32)
    t_ref[:zp, :] = jnp.zeros((zp, t_ref.shape[1]), F32)
    z_ref[zp:zp + n_meta, :] = um_ref[...]
    z_ref[zp + n_meta:, :] = u_ref[...]

    def window_sum(win):
        src, dst = z_ref, t_ref
        d = 1
        while d < win:
            dst[zp:end, :] = src[zp:end, :] + src[zp - d:end - d, :]
            src, dst = dst, src
            d *= 2
        return src[zp + n_meta:end, :]

    for gi, win in enumerate(POOL_WINDOWS):
        @pl.when((g == gi) & is_prompt)
        def _(win=win):
            mix = window_sum(win) / float(win) - u_ref[...]
            y = jnp.dot(mix.astype(BF16), w_ref[0], preferred_element_type=F32)
            o_ref[...] = (y * sc_ref[...]).astype(o_ref.dtype)


def _pool_extra_kernel(st_ref, us_ref, um_ref, w_ref, sc_ref, o_any, o_ref, acc_ref, zz_ref,
                       *, n_state, pg, n_meta_pad):
    del o_any
    j = pl.program_id(0)

    @pl.when(j == 0)
    def _():
        acc_ref[...] = jnp.zeros_like(acc_ref)

    for gi, win in enumerate(POOL_WINDOWS):
        cols = slice(gi * pg, (gi + 1) * pg)

        @pl.when(n_state - j <= win - 1)
        def _(cols=cols):
            acc_ref[:, cols] += st_ref[0, :, cols]

    @pl.when(j == n_state - 1)
    def _():
        zpad = zz_ref.shape[0] - n_meta_pad
        zz_ref[:zpad, :] = jnp.zeros((zpad, zz_ref.shape[1]), F32)
        zz_ref[zpad:, :] = um_ref[...]
        t = lax.broadcasted_iota(jnp.int32, (n_meta_pad, 1), 0)
        for gi, win in enumerate(POOL_WINDOWS):
            cols = slice(gi * pg, (gi + 1) * pg)
            us = us_ref[:, cols]
            mix_s = (acc_ref[:, cols] + us) / float(win) - us
            wsum = zz_ref[zpad:, cols]
            for kk in range(1, win):
                wsum = wsum + zz_ref[zpad - kk:zpad - kk + n_meta_pad, cols]
            cnt = jnp.minimum(t + 1, win).astype(F32)
            mix_m = wsum / cnt - um_ref[:, cols]
            w = w_ref[gi]
            sc = sc_ref[:, cols]
            o_ref[:n_meta_pad, cols] = (jnp.dot(mix_m.astype(BF16), w, preferred_element_type=F32)
                                        * sc).astype(o_ref.dtype)
            o_ref[n_meta_pad:, cols] = (jnp.dot(mix_s.astype(BF16), w, preferred_element_type=F32)
                                        * sc).astype(o_ref.dtype)


def _conv3(x0, x1, x2, w_ref, b_ref):
    return b_ref[...] + x0 * w_ref[0:1, :] + x1 * w_ref[1:2, :] + x2 * w_ref[2:3, :]


def _ffn_up_kernel(x_ref, xm_ref, wg_ref, wv_ref, cwg_ref, cwv_ref, cbg_ref, cbv_ref,
                   s0g_ref, s0v_ref, s1g_ref, s1v_ref,
                   act_ref, tailg_ref, tailv_ref, sampg_ref, sampv_ref,
                   xs_ref, upg_a, upv_a, upg_b, upv_b, eg_ref, ev_ref, prevg_ref, prevv_ref,
                   *, bm, n_cols, n_meta, samp_lo, n_samp, tails, starts):
    i, j = pl.program_id(0), pl.program_id(1)
    last_tile = pl.num_programs(0) - 1

    @pl.when(j == 0)
    def _():
        xs_ref[:bm, :] = x_ref[...]
        xs_ref[bm:, :] = xm_ref[...]

    @pl.when((i == 0) & (j == 0))
    def _():
        prevg_ref[...] = jnp.zeros(prevg_ref.shape, F32)
        prevv_ref[...] = jnp.zeros(prevv_ref.shape, F32)

    k_chunk = xs_ref.shape[1] // N_K_CHUNKS
    groups = bm // 16
    piece_rows = [(groups // N_ROW_PIECES + (1 if p < groups % N_ROW_PIECES else 0)) * 16
                  for p in range(N_ROW_PIECES)]
    piece_start = [sum(piece_rows[:p]) for p in range(N_ROW_PIECES)]

    def product_chunk(c, acc):
        ks = slice(c * k_chunk, (c + 1) * k_chunk)
        xk = xs_ref[:, ks]
        pg = jnp.dot(xk, wg_ref[ks, :].astype(BF16), preferred_element_type=F32)
        pv = jnp.dot(xk, wv_ref[ks, :].astype(BF16), preferred_element_type=F32)
        return (pg, pv) if acc is None else (acc[0] + pg, acc[1] + pv)

    def finish_rows(p, up_g, up_v):
        jj = j - 1
        r0, n = piece_start[p], piece_rows[p]
        r1 = r0 + n

        def conv(up_ref, e_ref, prev_ref, cw_ref, cb_ref):
            before = prev_ref[jj] if r0 == 0 else up_ref[r0 - 8:r0, :]
            cur = up_ref[r0:r1, :]
            if r1 == bm:
                prev_ref[jj] = up_ref[bm - 8:bm, :]
            ext = jnp.concatenate([before, cur], axis=0)
            xm1 = pltpu.roll(ext, 1, 0)[8:, :]
            xm2 = pltpu.roll(ext, 2, 0)[8:, :]
            return _conv3(xm2, xm1, cur, cw_ref, cb_ref)

        gate = conv(up_g, eg_ref, prevg_ref, cwg_ref, cbg_ref)
        val = conv(up_v, ev_ref, prevv_ref, cwv_ref, cbv_ref)
        act_ref[r0:r1, :] = (jax.nn.silu(gate) * val).astype(act_ref.dtype)

    def redo_sequence_start(lo, after_meta, up_g, up_v):
        def conv(up_ref, e_ref, cw_ref, cb_ref):
            if after_meta:
                e_ref[0:8, :] = up_ref[bm + n_meta - 8:bm + n_meta, :]
            else:
                e_ref[0:8, :] = jnp.zeros((8, e_ref.shape[1]), F32)
            e_ref[8:16, :] = up_ref[lo:lo + 8, :]
            return _conv3(e_ref[6:14, :], e_ref[7:15, :], up_ref[lo:lo + 8, :], cw_ref, cb_ref)

        gate = conv(up_g, eg_ref, cwg_ref, cbg_ref)
        val = conv(up_v, ev_ref, cwv_ref, cbv_ref)
        act_ref[lo:lo + 8, :] = (jax.nn.silu(gate) * val).astype(act_ref.dtype)

    def run(write, read):
        acc = None
        per = N_K_CHUNKS // N_ROW_PIECES
        for c in range(N_K_CHUNKS):
            if write is not None:
                acc = product_chunk(c, acc)
            if read is not None and c % per == 0:
                finish_rows(c // per, *read)
        if write is not None:
            write[0][...] = acc[0]
            write[1][...] = acc[1]
        if read is not None:
            finish_special_rows(*read)

    def finish_special_rows(up_g, up_v):
        for tile, lo, after_meta in starts:
            @pl.when(i == tile)
            def _(lo=lo, after_meta=after_meta):
                redo_sequence_start(lo, after_meta, up_g, up_v)

        samp = slice(samp_lo, samp_lo + n_samp)

        @pl.when(i == last_tile)
        def _():
            gate_s = _conv3(s0g_ref[...], s1g_ref[...], up_g[samp, :], cwg_ref, cbg_ref)
            val_s = _conv3(s0v_ref[...], s1v_ref[...], up_v[samp, :], cwv_ref, cbv_ref)
            act_ref[samp, :] = (jax.nn.silu(gate_s) * val_s).astype(act_ref.dtype)
            sampg_ref[0] = up_g[samp, :]
            sampv_ref[0] = up_v[samp, :]

        @pl.when(i != last_tile)
        def _():
            sampg_ref[0] = jnp.zeros(sampg_ref.shape[1:], F32)
            sampv_ref[0] = jnp.zeros(sampv_ref.shape[1:], F32)

        has_tail = i == tails[0][0]
        for tile, lo in tails:
            has_tail = has_tail | (i == tile)

            @pl.when(i == tile)
            def _(lo=lo):
                tailg_ref[0] = up_g[lo:lo + 8, :]
                tailv_ref[0] = up_v[lo:lo + 8, :]

        @pl.when(jnp.logical_not(has_tail))
        def _():
            tailg_ref[0] = jnp.zeros(tailg_ref.shape[1:], F32)
            tailv_ref[0] = jnp.zeros(tailv_ref.shape[1:], F32)

    buffers = ((upg_a, upv_a), (upg_b, upv_b))
    for parity in (0, 1):
        write, read = buffers[parity], buffers[1 - parity]
        is_mine = j % 2 == parity
        if parity == 0:
            @pl.when(j == 0)
            def _(write=write):
                run(write, None)

        @pl.when(is_mine & (j > 0) & (j < n_cols))
        def _(write=write, read=read):
            run(write, read)

        if n_cols % 2 == parity:
            @pl.when(j == n_cols)
            def _(read=read):
                run(None, read)


def kernel(x_prompt, x_sample, cache_ckv, cache_kpe, state_pool, state_conv, page_table, meta_tokens,
           norm_mix, w_in, g_qnorm, w_uq, g_kvnorm, w_uk, w_uv, w_attn_branch, w_pool_grp, pool_scale,
           w_pool_branch, w_out, norm_ffn, w_up, conv_w, conv_b, w_down, norm_final):
    nb, s_len, d = x_prompt.shape
    db, dec_t, _ = x_sample.shape
    depth = w_in.shape[0]
    assert depth == 1 and dec_t == 1
    nm = meta_tokens.shape[0]
    _, c_dim, n_heads, d_nope = w_uk.shape
    v_dim = w_uv.shape[3]
    r_dim = cache_kpe.shape[3]
    half = r_dim // 2
    q_lora = g_qnorm.shape[1]
    pw = pool_scale.shape[1]
    n_grp = len(POOL_WINDOWS)
    pg = pw // n_grp
    n_state = state_pool.shape[2]
    conv_taps = conv_w.shape[1]
    dff2 = w_up.shape[2]
    dff = dff2 // 2
    page_size = cache_ckv.shape[2]
    n_pages = page_table.shape[1]
    past = n_pages * page_size
    scale = float(d_nope + r_dim) ** -0.5 * LOG2_E
    hd = 2 * LANES
    assert d_nope == LANES and v_dim == LANES and r_dim <= LANES and conv_taps == 3
    assert nm >= max(POOL_WINDOWS) - 1 and nm % 16 == 0 and nm <= META_PAD
    assert n_state == max(POOL_WINDOWS) - 1

    rt = nb * s_len
    r_meta = rt
    r_samp = rt + META_PAD
    e_rows = META_PAD + db
    m_rows = rt + e_rows
    assert rt % META_PAD == 0 and r_samp % db == 0 and db % 16 == 0 and r_meta % nm == 0

    off_q, off_kv, off_kr, off_gp = pw, pw + q_lora, pw + q_lora + c_dim, pw + q_lora + c_dim + r_dim
    assert off_q % q_lora == 0 and off_kv % c_dim == 0 and off_kr % LANES == 0 and pw % pg == 0

    n_a = off_kr + LANES
    n_a_pad = -(-n_a // 256) * 256
    w_a = jnp.concatenate([w_in[0, :, :off_gp], jnp.zeros((d, n_a_pad - off_gp), F32)], axis=1).astype(BF16)
    w_in_t = jnp.swapaxes(w_in[0], 0, 1)
    wq = w_uq[0].reshape(q_lora, n_heads, d_nope + r_dim)
    wq = jnp.concatenate([wq, jnp.zeros((q_lora, n_heads, hd - d_nope - r_dim), F32)], axis=2)
    wq = wq.reshape(q_lora, n_heads * hd).astype(BF16)
    w_uk2 = w_uk[0].reshape(c_dim, n_heads * d_nope).astype(BF16)
    w_ukt = jnp.transpose(w_uk[0], (1, 2, 0)).astype(BF16)
    w_uv2 = w_uv[0].reshape(c_dim, n_heads * v_dim).astype(BF16)
    w_pg = w_pool_grp[0].astype(BF16)
    w_d = w_down[0].astype(BF16)
    w_ab, w_pb, w_o, w_u = w_attn_branch[0], w_pool_branch[0], w_out[0], w_up[0]

    pos = jnp.concatenate([
        jnp.tile(nm + jnp.arange(s_len), nb),
        jnp.arange(nm), jnp.zeros((META_PAD - nm,), jnp.int32),
        jnp.full((db,), past, jnp.int32)])
    inv = ROPE_BASE ** (-jnp.arange(0, r_dim, 2, dtype=F32) / r_dim)
    ang = pos.astype(F32)[:, None] * inv[None, :]
    cos, sin = jnp.cos(ang), jnp.sin(ang)
    zl = jnp.zeros((m_rows, LANES - 2 * half), F32)
    zh = jnp.zeros((m_rows, half), F32)
    tab_c = jnp.concatenate([cos, cos, zl], axis=1)
    tab_s1 = jnp.concatenate([-sin, zh, zl], axis=1)
    tab_s2 = jnp.concatenate([zh, sin, zl], axis=1)

    bm = _pick_block(m_rows, 1100, 16)
    row_tab_spec = pl.BlockSpec((bm, LANES), lambda i, j: (i, 0))

    x_extra = jnp.concatenate([meta_tokens.astype(F32), jnp.zeros((META_PAD - nm, d), F32),
                               x_sample.reshape(db, d)], axis=0)
    rb = _pick_block(e_rows, 256, 16)
    assert rt % rb == 0
    x_all, h = _norm_in(x_prompt.reshape(rt, d), x_extra, norm_mix[0], rb)

    bn_a = _pick_block(n_a_pad, 768, LANES)
    u1 = _matmul(h, w_a, m=m_rows, k=d, n=n_a_pad, bm=bm, bn=bn_a, out_cols=n_a_pad, out_bn=bn_a,
                 out_dtype=F32, name="in_proj")

    def gate_epilogue(acc, o_ref):
        o_ref[...] = jax.nn.sigmoid(acc).astype(o_ref.dtype)

    bn_d = _pick_block(d, 512, LANES)
    gates = _matmul(h, w_in_t, m=m_rows, k=d, n=2 * d, bm=bm, bn=bn_d, out_cols=2 * d, out_bn=bn_d,
                    out_dtype=BF16, name="gate_proj", epilogue=gate_epilogue, w_nk_row0=off_gp)

    def q_prologue(x_ref, g_ref):
        return _rms(x_ref[...], g_ref[...])

    bn_q = _pick_block(n_heads * hd, 1024, hd)

    def q_epilogue(acc, o_ref, ct_ref, s1_ref, s2_ref):
        acc = acc * scale
        for cb in range(bn_q // LANES):
            cols = slice(cb * LANES, (cb + 1) * LANES)
            v = acc[:, cols]
            if cb % 2 == 1:
                v = _rope_tile(v, ct_ref[...], s1_ref[...], s2_ref[...], half)
            o_ref[:, cols] = v.astype(o_ref.dtype)

    q_all = _matmul(u1, wq, m=m_rows, k=q_lora, n=n_heads * hd, bm=bm, bn=bn_q,
                    out_cols=n_heads * hd, out_bn=bn_q, out_dtype=BF16, name="q_proj",
                    x_col_block=off_q // q_lora,
                    pro=(g_qnorm[0].reshape(1, q_lora),), pro_specs=(pl.BlockSpec((1, q_lora), lambda i, j: (0, 0)),),
                    prologue=q_prologue,
                    extra=(tab_c, tab_s1, tab_s2), extra_specs=(row_tab_spec,) * 3, epilogue=q_epilogue)

    tab1 = pl.BlockSpec((bm, LANES), lambda i: (i, 0))
    ckv32, ckv16, kpe32 = pl.pallas_call(
        functools.partial(_kv_prep_kernel, half=half),
        grid=(m_rows // bm,),
        in_specs=[pl.BlockSpec((bm, c_dim), lambda i: (i, off_kv // c_dim)),
                  pl.BlockSpec((bm, LANES), lambda i: (i, off_kr // LANES)),
                  pl.BlockSpec((1, c_dim), lambda i: (0, 0)),
                  tab1, tab1, tab1],
        out_specs=[pl.BlockSpec((bm, c_dim), lambda i: (i, 0)),
                   pl.BlockSpec((bm, c_dim), lambda i: (i, 0)),
                   pl.BlockSpec((bm, LANES), lambda i: (i, 0))],
        out_shape=[jax.ShapeDtypeStruct((m_rows, c_dim), F32),
                   jax.ShapeDtypeStruct((m_rows, c_dim), BF16),
                   jax.ShapeDtypeStruct((m_rows, LANES), F32)],
        compiler_params=_params("arbitrary"),
        name="kv_prep",
    )(u1, u1, g_kvnorm[0].reshape(1, c_dim), tab_c, tab_s1, tab_s2)

    hps = _pick_block(n_heads, 16, 1)

    def k_epilogue(acc, o_ref, kpe_ref):
        kr = kpe_ref[...].astype(o_ref.dtype)
        for hh in range(hps):
            o_ref[:, hh * hd:hh * hd + d_nope] = acc[:, hh * d_nope:(hh + 1) * d_nope].astype(o_ref.dtype)
            o_ref[:, hh * hd + d_nope:(hh + 1) * hd] = kr

    k_all = _matmul(ckv16, w_uk2, m=m_rows, k=c_dim, n=n_heads * d_nope, bm=bm, bn=hps * d_nope,
                    out_cols=n_heads * hd, out_bn=hps * hd, out_dtype=BF16, name="k_up",
                    extra=(kpe32,), extra_specs=(row_tab_spec,), epilogue=k_epilogue)
    bn_v = _pick_block(n_heads * v_dim, 2048, LANES)
    v_all = _matmul(ckv16, w_uv2, m=m_rows, k=c_dim, n=n_heads * v_dim, bm=bm, bn=bn_v,
                    out_cols=n_heads * v_dim, out_bn=bn_v, out_dtype=BF16, name="v_up")

    bq = _pick_block(s_len, 512, 128)
    mb = r_meta // META_PAD
    assert e_rows <= s_len

    def prompt_block(b, col, col_off=0):
        return jnp.minimum(b, nb - 1), jnp.where(b < nb, col, 0) + col_off
    o_all = pl.pallas_call(
        functools.partial(_attn_tok_kernel, bq=bq, n_meta=nm, n_prompts=nb),
        grid=(nb + 1, n_heads),
        in_specs=[pl.BlockSpec((s_len, hd), prompt_block),
                  pl.BlockSpec((s_len, hd), prompt_block),
                  pl.BlockSpec((s_len, v_dim), prompt_block),
                  pl.BlockSpec((META_PAD, hd), lambda b, hh: (mb, hh)),
                  pl.BlockSpec((META_PAD, v_dim), lambda b, hh: (mb, hh))],
        out_specs=pl.BlockSpec((s_len, v_dim), lambda b, hh: (b, hh)),
        out_shape=jax.ShapeDtypeStruct((m_rows, n_heads * v_dim), BF16),
        compiler_params=_params("parallel", "arbitrary"),
        name="attn_prompt",
    )(q_all, k_all, v_all, k_all, v_all)

    o_all = pl.pallas_call(
        functools.partial(_attn_meta_kernel, n_meta=nm),
        grid=(n_heads,),
        in_specs=[pl.BlockSpec((META_PAD, hd), lambda hh: (mb, hh)),
                  pl.BlockSpec((META_PAD, hd), lambda hh: (mb, hh)),
                  pl.BlockSpec((META_PAD, v_dim), lambda hh: (mb, hh)),
                  pl.BlockSpec(memory_space=pl.ANY)],
        out_specs=pl.BlockSpec((META_PAD, v_dim), lambda hh: (mb, hh)),
        out_shape=jax.ShapeDtypeStruct((m_rows, n_heads * v_dim), BF16),
        input_output_aliases={3: 0},
        compiler_params=_params("arbitrary"),
        name="attn_meta",
    )(q_all, k_all, v_all, o_all)

    sb = r_samp // db
    q_lat = pl.pallas_call(
        _qlat_kernel,
        grid=(n_heads,),
        in_specs=[pl.BlockSpec((db, d_nope), lambda hh: (sb, hh * (hd // d_nope))),
                  pl.BlockSpec((1, d_nope, c_dim), lambda hh: (hh, 0, 0))],
        out_specs=pl.BlockSpec((1, db, c_dim), lambda hh: (hh, 0, 0)),
        out_shape=jax.ShapeDtypeStruct((n_heads, db, c_dim), F32),
        compiler_params=_params("arbitrary"),
        name="q_absorb",
    )(q_all, w_ukt)
    qk_dim = -(-(c_dim + r_dim) // LANES) * LANES
    q_samp = q_all[r_samp:r_samp + db].reshape(db, n_heads, hd)
    q_cat = jnp.concatenate([jnp.transpose(q_lat, (1, 0, 2)).astype(BF16),
                             q_samp[:, :, d_nope:d_nope + r_dim],
                             jnp.zeros((db, n_heads, qk_dim - c_dim - r_dim), BF16)], axis=2)
    new_cat = jnp.concatenate([ckv32[r_samp:r_samp + db], kpe32[r_samp:r_samp + db, :r_dim],
                               jnp.zeros((db, qk_dim - c_dim - r_dim), F32)], axis=1).reshape(db, 1, qk_dim)
    pages = _pick_block(n_pages, 32, 1)
    streams = _pick_block(pages, 4, 1)
    cache_kpe_t = jnp.swapaxes(cache_kpe[0], 1, 2)
    o_lat = _decode_attention(page_table, cache_ckv[0], cache_kpe_t, q_cat, new_cat, pages=pages, streams=streams)
    o_lat_t = jnp.transpose(o_lat, (1, 0, 2)).astype(BF16)
    o_all = pl.pallas_call(
        _ouv_kernel,
        grid=(n_heads,),
        in_specs=[pl.BlockSpec((1, db, c_dim), lambda hh: (hh, 0, 0)),
                  pl.BlockSpec((c_dim, v_dim), lambda hh: (0, hh)),
                  pl.BlockSpec(memory_space=pl.ANY)],
        out_specs=pl.BlockSpec((db, v_dim), lambda hh: (sb, hh)),
        out_shape=jax.ShapeDtypeStruct((m_rows, n_heads * v_dim), BF16),
        input_output_aliases={2: 0},
        compiler_params=_params("arbitrary"),
        name="o_up_decode",
    )(o_lat_t, w_uv2, o_all)

    scale_row = pool_scale[0].reshape(1, pw)
    pm = pl.pallas_call(
        functools.partial(_pool_tok_kernel, n_meta=nm, n_prompts=nb),
        grid=(nb + 1, n_grp),
        in_specs=[pl.BlockSpec((s_len, pg), prompt_block),
                  pl.BlockSpec((nm, pg), lambda b, g: (r_meta // nm, g)),
                  pl.BlockSpec((1, pg, pg), lambda b, g: (g, 0, 0)),
                  pl.BlockSpec((1, pg), lambda b, g: (0, g))],
        out_specs=pl.BlockSpec((s_len, pg), lambda b, g: (b, g)),
        out_shape=jax.ShapeDtypeStruct((m_rows, pw), BF16),
        scratch_shapes=[pltpu.VMEM((POOL_ZERO_ROWS + nm + s_len, pg), F32),
                        pltpu.VMEM((POOL_ZERO_ROWS + nm + s_len, pg), F32)],
        compiler_params=_params("parallel", "arbitrary"),
        name="pool_prompt",
    )(u1, u1, w_pg, scale_row)
    st_rows = jnp.transpose(state_pool[0], (1, 0, 2))
    zrows = POOL_ZERO_ROWS
    pm = pl.pallas_call(
        functools.partial(_pool_extra_kernel, n_state=n_state, pg=pg, n_meta_pad=META_PAD),
        grid=(n_state,),
        in_specs=[pl.BlockSpec((1, db, pw), lambda j: (j, 0, 0)),
                  pl.BlockSpec((db, pw), lambda j: (sb, 0)),
                  pl.BlockSpec((META_PAD, pw), lambda j: (mb, 0)),
                  pl.BlockSpec((n_grp, pg, pg), lambda j: (0, 0, 0)),
                  pl.BlockSpec((1, pw), lambda j: (0, 0)),
                  pl.BlockSpec(memory_space=pl.ANY)],
        out_specs=pl.BlockSpec((e_rows, pw), lambda j: (rt // e_rows, 0)),
        out_shape=jax.ShapeDtypeStruct((m_rows, pw), BF16),
        scratch_shapes=[pltpu.VMEM((db, pw), F32), pltpu.VMEM((zrows + META_PAD, pw), F32)],
        input_output_aliases={5: 0},
        compiler_params=_params("arbitrary"),
        name="pool_extra",
    )(st_rows, u1, u1, w_pg, scale_row, pm)

    def pool_branch_epilogue(acc, o_ref, g_ref):
        o_ref[...] = (g_ref[...].astype(F32) * acc).astype(o_ref.dtype)

    tile_spec = pl.BlockSpec((bm, bn_d), lambda i, j: (i, j))
    t1 = _matmul(pm, w_pb, m=m_rows, k=pw, n=d, bm=bm, bn=bn_d, out_cols=d, out_bn=bn_d, out_dtype=BF16,
                 name="pool_branch", extra=(gates,), extra_specs=(tile_spec,), epilogue=pool_branch_epilogue)

    def attn_branch_epilogue(acc, o_ref, g_ref, t_ref):
        o_ref[...] = (t_ref[...].astype(F32) + g_ref[...].astype(F32) * acc).astype(o_ref.dtype)

    mixed = _matmul(o_all, w_ab, m=m_rows, k=n_heads * v_dim, n=d, bm=bm, bn=bn_d, out_cols=d, out_bn=bn_d,
                    out_dtype=BF16, name="attn_branch", extra=(gates, t1),
                    extra_specs=(pl.BlockSpec((bm, bn_d), lambda i, j: (i, j + d // bn_d)), tile_spec),
                    epilogue=attn_branch_epilogue)

    def residual_epilogue(acc, o_ref, r_ref):
        o_ref[...] = r_ref[...] + acc

    x1 = _matmul(mixed, w_o, m=m_rows, k=d, n=d, bm=bm, bn=bn_d, out_cols=d, out_bn=bn_d, out_dtype=F32,
                 name="out_proj", extra=(x_all,), extra_specs=(tile_spec,), epilogue=residual_epilogue)

    rb2 = _pick_block(m_rows, 512, 16)
    h2 = _norm_rows(x1, norm_ffn[0], row0=0, nrows=m_rows, rb=rb2, out_dtype=BF16, name="norm_ffn")
    bc = _pick_block(dff, 256, LANES)
    ncb = dff // bc
    n_mt = m_rows // bm
    last_mt = n_mt - 1
    samp_lo = r_samp - last_mt * bm
    assert samp_lo >= 0 and r_samp + db == m_rows
    tails = tuple(divmod((b + 1) * s_len - 8, bm) for b in range(nb))
    assert len({t for t, _ in tails}) == nb and all(lo + 8 <= bm for _, lo in tails)
    starts = tuple((*divmod(b * s_len, bm), True) for b in range(nb)) + ((*divmod(rt, bm), False),)
    assert nm >= 8 and all(lo % 8 == 0 for _, lo, _ in starts)
    cw, cb = conv_w[0], conv_b[0].reshape(1, dff2)
    sc2d = state_conv[0].reshape(db, (conv_taps - 1) * dff2)

    def done_block(j):
        return jnp.maximum(j - 1, 0)

    def state_spec(off):
        return pl.BlockSpec((db, bc), lambda i, j: (0, jnp.where(i == last_mt, done_block(j), 0) + off))

    samp_spec = pl.BlockSpec((1, db, bc), lambda i, j: (jnp.where(i == last_mt, last_mt + done_block(j), i), 0, 0))
    tail_spec = pl.BlockSpec((1, 8, bc), lambda i, j: (i, 0, done_block(j)))
    act, tail_g, tail_v, samp_g, samp_v = pl.pallas_call(
        functools.partial(_ffn_up_kernel, bm=bm, n_cols=ncb, n_meta=nm,
                          samp_lo=samp_lo, n_samp=db, tails=tails, starts=starts),
        grid=(n_mt, ncb + 1),
        in_specs=[pl.BlockSpec((bm, d), lambda i, j: (i, 0), pipeline_mode=pl.Buffered(1)),
                  pl.BlockSpec((nm, d), lambda i, j: (r_meta // nm, 0)),
                  pl.BlockSpec((d, bc), lambda i, j: (0, jnp.minimum(j, ncb - 1))),
                  pl.BlockSpec((d, bc), lambda i, j: (0, jnp.minimum(j, ncb - 1) + ncb)),
                  pl.BlockSpec((conv_taps, bc), lambda i, j: (0, done_block(j))),
                  pl.BlockSpec((conv_taps, bc), lambda i, j: (0, done_block(j) + ncb)),
                  pl.BlockSpec((1, bc), lambda i, j: (0, done_block(j))),
                  pl.BlockSpec((1, bc), lambda i, j: (0, done_block(j) + ncb)),
                  state_spec(0), state_spec(ncb), state_spec(2 * ncb), state_spec(3 * ncb)],
        out_specs=[pl.BlockSpec((bm, bc), lambda i, j: (i, done_block(j))),
                   tail_spec, tail_spec, samp_spec, samp_spec],
        out_shape=[jax.ShapeDtypeStruct((m_rows, dff), BF16),
                   jax.ShapeDtypeStruct((n_mt, 8, dff), F32),
                   jax.ShapeDtypeStruct((n_mt, 8, dff), F32),
                   jax.ShapeDtypeStruct((last_mt + ncb, db, bc), F32),
                   jax.ShapeDtypeStruct((last_mt + ncb, db, bc), F32)],
        scratch_shapes=[pltpu.VMEM((bm + nm, d), BF16),
                        pltpu.VMEM((bm + nm, bc), F32), pltpu.VMEM((bm + nm, bc), F32),
                        pltpu.VMEM((bm + nm, bc), F32), pltpu.VMEM((bm + nm, bc), F32),
                        pltpu.VMEM((8 + bm, bc), F32), pltpu.VMEM((8 + bm, bc), F32),
                        pltpu.VMEM((ncb, 8, bc), F32), pltpu.VMEM((ncb, 8, bc), F32)],
        compiler_params=_params("arbitrary", "arbitrary"),
        name="ffn_up",
    )(h2, h2, w_u, w_u, cw, cw, cb, cb, sc2d, sc2d, sc2d, sc2d)
    n_tail = conv_taps - 1
    tail_tiles = jnp.array([t for t, _ in tails])
    up_tail = jnp.concatenate([tail_g[tail_tiles, 8 - n_tail:], tail_v[tail_tiles, 8 - n_tail:]], axis=2)

    def sample_cols(a):
        return jnp.transpose(a[last_mt:], (1, 0, 2)).reshape(db, dff)

    up_samp = jnp.concatenate([sample_cols(samp_g), sample_cols(samp_v)], axis=1)

    bm_d = _pick_block(m_rows, 528, 16)
    bn_dn = _pick_block(d, 512, LANES)
    x2 = _matmul(act, w_d, m=m_rows, k=dff, n=d, bm=bm_d, bn=bn_dn, out_cols=d, out_bn=bn_dn, out_dtype=F32,
                 name="ffn_down", extra=(x1,), extra_specs=(pl.BlockSpec((bm_d, bn_dn), lambda i, j: (i, j)),),
                 epilogue=residual_epilogue)

    rb3 = _pick_block(rt, 512, 16)
    y_prompt = _norm_rows(x2, norm_final, row0=0, nrows=rt, rb=rb3, out_dtype=F32, name="norm_final_prompt")
    y_sample = _norm_rows(x2, norm_final, row0=r_samp, nrows=db, rb=db, out_dtype=F32, name="norm_final_sample")

    def with_meta(a, width):
        tok = a[:rt, :width].reshape(nb, s_len, width)
        meta = jnp.broadcast_to(a[r_meta:r_meta + nm, :width][None], (nb, nm, width))
        return jnp.concatenate([meta, tok], axis=1)[None]

    def prompt_tail(a, n_tail, width):
        return jnp.stack([a[(b + 1) * s_len - n_tail:(b + 1) * s_len, :width] for b in range(nb)])[None]

    return (
        y_prompt.reshape(nb, s_len, d),
        y_sample.reshape(db, 1, d),
        with_meta(ckv32, c_dim),
        with_meta(kpe32, r_dim),
        prompt_tail(u1, n_state, pw),
        up_tail[None],
        ckv32[r_samp:r_samp + db].reshape(1, db, 1, c_dim),
        kpe32[r_samp:r_samp + db, :r_dim].reshape(1, db, 1, r_dim),
        jnp.concatenate([state_pool[0, :, 1:], u1[r_samp:r_samp + db, :pw][:, None]], axis=1)[None],
        jnp.concatenate([state_conv[0, :, 1:], up_samp[:, None]], axis=1)[None],
    )
```
